```python
import math
import jax, jax.numpy as jnp
from jax import lax
import numpy as np

D_MODEL = 1024
BATCH = 8
SEQ = 8192
DEPTH = 1

RET_HEADS = 4
RET_QK_DIM = 128
RET_V_DIM = 256
RET_CHUNK = 128
RET_QK_WIDTH = RET_HEADS * RET_QK_DIM
RET_V_WIDTH = RET_HEADS * RET_V_DIM
ROPE_BASE = 10000.0
SSM_GROUP_CH = 16
SSM_GROUPS = 32
SSM_WIDTH = SSM_GROUPS * SSM_GROUP_CH
SSM_STATE = 64
DT_MIN = 1e-3
DT_MAX = 1e-1
FFN_HIDDEN = -(-8 * D_MODEL // (3 * 256)) * 256
IN_WIDTH = 2 * RET_QK_WIDTH + 2 * RET_V_WIDTH + SSM_WIDTH + 2 * D_MODEL
IN_SPLITS = (
    RET_QK_WIDTH,
    2 * RET_QK_WIDTH,
    2 * RET_QK_WIDTH + RET_V_WIDTH,
    2 * RET_QK_WIDTH + 2 * RET_V_WIDTH,
    2 * RET_QK_WIDTH + 2 * RET_V_WIDTH + SSM_WIDTH,
)
EPS = 1e-6

kernel_name = "hybrid_retention_s5_gated_block"


def rms_norm(x, g):
    xf = x.astype(jnp.float32)
    xf = xf * lax.rsqrt(jnp.mean(xf * xf, axis=-1, keepdims=True) + EPS)
    return xf.astype(x.dtype) * g


def rotary(x, positions):
    d = x.shape[-1]
    inv_freq = ROPE_BASE ** (-jnp.arange(0, d, 2, dtype=jnp.float32) / d)
    ang = positions.astype(jnp.float32)[..., None] * inv_freq
    cos = jnp.cos(ang)[:, :, None, :].astype(x.dtype)
    sin = jnp.sin(ang)[:, :, None, :].astype(x.dtype)
    x1, x2 = jnp.split(x, 2, axis=-1)
    return jnp.concatenate([x1 * cos - x2 * sin, x1 * sin + x2 * cos], axis=-1)


def retention_chunkwise(q, k, v):
    b, l, h, dk = q.shape
    dv = v.shape[-1]
    c = RET_CHUNK
    n = l // c
    dt = q.dtype
    log_g = jnp.log(1.0 - 2.0 ** (-5.0 - jnp.arange(h, dtype=jnp.float32)))
    idx = jnp.arange(c, dtype=jnp.float32)
    diff = idx[:, None] - idx[None, :]
    inner_decay = jnp.where(diff >= 0, jnp.exp(log_g[:, None, None] * jnp.maximum(diff, 0.0)), 0.0).astype(dt)
    zeta = jnp.exp(log_g[None, :] * (c - 1.0 - idx)[:, None]).astype(dt)
    xi = jnp.exp(log_g[None, :] * (idx + 1.0)[:, None]).astype(dt)
    chunk_decay = jnp.exp(log_g * c).astype(dt)

    qc = q.reshape(b, n, c, h, dk)
    kc = k.reshape(b, n, c, h, dk)
    vc = v.reshape(b, n, c, h, dv)

    scores = jnp.einsum('bnihd,bnjhd->bnhij', qc, kc) * inner_decay[None, None]
    inner = jnp.einsum('bnhij,bnjhe->bnihe', scores, vc)

    kv = jnp.einsum('bnjhd,bnjhe->bnhde', kc, vc * zeta[None, None, :, :, None])

    def step(state, kv_n):
        return chunk_decay[None, :, None, None] * state + kv_n, state

    _, r_prev = lax.scan(step, jnp.zeros_like(kv[:, 0]), jnp.moveaxis(kv, 1, 0))
    r_prev = jnp.moveaxis(r_prev, 0, 1)
    cross = jnp.einsum('bnihd,bnhde->bnihe', qc * xi[None, None, :, :, None], r_prev)
    return (inner + cross).reshape(b, l, h, dv)


def head_group_norm(o):
    of = o.astype(jnp.float32)
    mu = jnp.mean(of, axis=-1, keepdims=True)
    var = jnp.mean(jnp.square(of - mu), axis=-1, keepdims=True)
    return ((of - mu) * lax.rsqrt(var + EPS)).astype(o.dtype)


def s5_mimo(u, a_re, a_im, log_dt, b_re, b_im, c_re, c_im, d_skip):
    bsz, l, _ = u.shape
    ug = u.reshape(bsz, l, SSM_GROUPS, SSM_GROUP_CH)
    dt = jnp.exp(log_dt)[:, None]
    da_re = dt * a_re
    da_im = dt * a_im
    mag = jnp.exp(da_re)
    ab_re = mag * jnp.cos(da_im)
    ab_im = mag * jnp.sin(da_im)
    den = a_re * a_re + a_im * a_im
    num_re = ab_re - 1.0
    f_re = (num_re * a_re + ab_im * a_im) / den
    f_im = (ab_im * a_re - num_re * a_im) / den
    bb_re = f_re[:, :, None] * b_re - f_im[:, :, None] * b_im
    bb_im = f_re[:, :, None] * b_im + f_im[:, :, None] * b_re
    bu_re = jnp.einsum('gpc,blgc->blgp', bb_re, ug)
    bu_im = jnp.einsum('gpc,blgc->blgp', bb_im, ug)
    shape_a = (1, l, SSM_GROUPS, SSM_STATE)
    a_seq_re = jnp.broadcast_to(ab_re[None, None], shape_a)
    a_seq_im = jnp.broadcast_to(ab_im[None, None], shape_a)

    def combine(e1, e2):
        a1r, a1i, b1r, b1i = e1
        a2r, a2i, b2r, b2i = e2
        return (a2r * a1r - a2i * a1i,
                a2r * a1i + a2i * a1r,
                a2r * b1r - a2i * b1i + b2r,
                a2r * b1i + a2i * b1r + b2i)

    _, _, x_re, x_im = lax.associative_scan(combine, (a_seq_re, a_seq_im, bu_re, bu_im), axis=1)
    y = jnp.einsum('gcp,blgp->blgc', c_re, x_re) - jnp.einsum('gcp,blgp->blgc', c_im, x_im)
    y = y + d_skip.reshape(SSM_GROUPS, SSM_GROUP_CH) * ug
    return y.reshape(bsz, l, SSM_WIDTH)


def setup_inputs(seed: int = 0) -> dict:
    key = jax.random.key(seed)
    ks = jax.random.split(key, 24)
    f32 = jnp.float32

    def nrm(k, shape, scale):
        return jax.random.normal(k, shape, f32) * scale

    def gain(k):
        return 1.0 + 0.02 * jax.random.normal(k, (DEPTH, D_MODEL), f32)

    x = jax.random.normal(ks[0], (BATCH, SEQ, D_MODEL), f32)
    offset = jax.random.randint(ks[1], (BATCH,), 0, 4096, dtype=jnp.int32)
    positions = offset[:, None] + jnp.arange(SEQ, dtype=jnp.int32)[None, :]
    n_idx = jnp.arange(SSM_STATE, dtype=f32)
    ssm_a_re = -0.5 + 0.01 * jax.random.normal(ks[2], (DEPTH, SSM_GROUPS, SSM_STATE), f32)
    ssm_a_im = jnp.broadcast_to(math.pi * n_idx, (DEPTH, SSM_GROUPS, SSM_STATE))
    ssm_log_dt = jax.random.uniform(ks[3], (DEPTH, SSM_GROUPS), f32, math.log(DT_MIN), math.log(DT_MAX))
    b_scale = (2.0 * SSM_GROUP_CH) ** -0.5
    c_scale = (2.0 * SSM_STATE) ** -0.5
    return {
        "x": x,
        "positions": positions,
        "mix_pre_norm": gain(ks[4]),
        "w_in": nrm(ks[5], (DEPTH, D_MODEL, IN_WIDTH), D_MODEL ** -0.5),
        "ssm_a_re": ssm_a_re,
        "ssm_a_im": ssm_a_im,
        "ssm_log_dt": ssm_log_dt,
        "ssm_b_re": nrm(ks[6], (DEPTH, SSM_GROUPS, SSM_STATE, SSM_GROUP_CH), b_scale),
        "ssm_b_im": nrm(ks[7], (DEPTH, SSM_GROUPS, SSM_STATE, SSM_GROUP_CH), b_scale),
        "ssm_c_re": nrm(ks[8], (DEPTH, SSM_GROUPS, SSM_GROUP_CH, SSM_STATE), c_scale),
        "ssm_c_im": nrm(ks[9], (DEPTH, SSM_GROUPS, SSM_GROUP_CH, SSM_STATE), c_scale),
        "ssm_d": nrm(ks[10], (DEPTH, SSM_WIDTH), 1.0),
        "w_glu_val": nrm(ks[11], (DEPTH, SSM_WIDTH, D_MODEL), SSM_WIDTH ** -0.5),
        "w_glu_gate": nrm(ks[12], (DEPTH, SSM_WIDTH, D_MODEL), SSM_WIDTH ** -0.5),
        "w_ret_up": nrm(ks[13], (DEPTH, RET_V_WIDTH, D_MODEL), RET_V_WIDTH ** -0.5),
        "w_out": nrm(ks[14], (DEPTH, D_MODEL, D_MODEL), D_MODEL ** -0.5),
        "mix_post_norm": gain(ks[15]),
        "ffn_pre_norm": gain(ks[16]),
        "w_ffn_gate": nrm(ks[17], (DEPTH, D_MODEL, FFN_HIDDEN), D_MODEL ** -0.5),
        "w_ffn_up": nrm(ks[18], (DEPTH, D_MODEL, FFN_HIDDEN), D_MODEL ** -0.5),
        "w_ffn_down": nrm(ks[19], (DEPTH, FFN_HIDDEN, D_MODEL), FFN_HIDDEN ** -0.5),
        "ffn_post_norm": gain(ks[20]),
    }


def reference(x, positions, mix_pre_norm, w_in, ssm_a_re, ssm_a_im, ssm_log_dt, ssm_b_re, ssm_b_im,
              ssm_c_re, ssm_c_im, ssm_d, w_glu_val, w_glu_gate, w_ret_up, w_out, mix_post_norm,
              ffn_pre_norm, w_ffn_gate, w_ffn_up, w_ffn_down, ffn_post_norm):
    bsz, l, _ = x.shape
    h = x
    for layer in range(DEPTH):
        u = rms_norm(h, mix_pre_norm[layer])
        proj = u @ w_in[layer]
        q, k, v, g_ret, u_ssm, g_merge = jnp.split(proj, IN_SPLITS, axis=-1)

        q = rotary(q.reshape(bsz, l, RET_HEADS, RET_QK_DIM), positions)
        k = rotary(k.reshape(bsz, l, RET_HEADS, RET_QK_DIM), positions) * (RET_QK_DIM ** -0.5)
        v = v.reshape(bsz, l, RET_HEADS, RET_V_DIM)
        ret = head_group_norm(retention_chunkwise(q, k, v)).reshape(bsz, l, RET_V_WIDTH)
        y_a = (jax.nn.silu(g_ret) * ret) @ w_ret_up[layer]

        y_s = s5_mimo(u_ssm, ssm_a_re[layer], ssm_a_im[layer], ssm_log_dt[layer], ssm_b_re[layer],
                      ssm_b_im[layer], ssm_c_re[layer], ssm_c_im[layer], ssm_d[layer])
        z = jax.nn.gelu(y_s)
        y_b = (z @ w_glu_val[layer]) * jax.nn.sigmoid(z @ w_glu_gate[layer])

        gate_a, gate_b = jnp.split(jax.nn.sigmoid(g_merge), 2, axis=-1)
        mixed = (gate_a * y_a + gate_b * y_b) @ w_out[layer]
        h = h + rms_norm(mixed, mix_post_norm[layer])

        f_in = rms_norm(h, ffn_pre_norm[layer])
        f = (jax.nn.silu(f_in @ w_ffn_gate[layer]) * (f_in @ w_ffn_up[layer])) @ w_ffn_down[layer]
        h = h + rms_norm(f, ffn_post_norm[layer])
    return h
```

```python
import functools
import math

import jax
import jax.numpy as jnp
from jax import lax
from jax.experimental import pallas as pl
from jax.experimental.pallas import tpu as pltpu

F32 = jnp.float32
BF16 = jnp.bfloat16

EPS = 1e-6
ROPE_BASE = 10000.0
RET_HEADS = 4
RET_QK_DIM = 128
RET_V_DIM = 256
RET_CHUNK = 128
SSM_GROUP_CH = 16
SSM_GROUPS = 32
SSM_STATE = 64
SSM_CHUNK = 16
SSM_CHUNK_W = SSM_CHUNK * SSM_GROUP_CH
LANES = 128
GROUPS_PER_LANE_TILE = LANES // SSM_GROUP_CH
VMEM_LIMIT_BYTES = 58 * 1024 * 1024

MIXER_TILE = 512
FFN_TILE = 512


def _dot(a, b):
    return jnp.dot(a, b, preferred_element_type=F32)


def _dot_nt(a, b):
    return lax.dot_general(a, b, (((1,), (1,)), ((), ())), preferred_element_type=F32)


def _rms_scale(x):
    return lax.rsqrt(jnp.mean(x * x, axis=-1, keepdims=True) + EPS)


def _transpose8(pieces, lane_block):
    a = list(pieces)
    s = 4
    while s >= 1:
        keep = (lane_block & s) == 0
        shift = SSM_GROUP_CH * s
        for i in range(8):
            if i & s:
                continue
            lo, hi = a[i], a[i + s]
            a[i] = jnp.where(keep, lo, pltpu.roll(hi, shift, 1))
            a[i + s] = jnp.where(keep, pltpu.roll(lo, LANES - shift, 1), hi)
        s //= 2
    return a


def _mixer_kernel(x_ref, pos_ref, invf_ref, gpre_ref, win_ref, dec_ref, xi_ref, zeta_ref, cd_ref,
                  wret_ref, m_ref, wst_ref, cout_ref, lama_ref, lamb_ref, wglu_ref, wout_ref,
                  gpost_ref, o_ref,
                  rstate, sv, sw, q_scr, k_scr, v_scr, sg_scr, gated_scr, ussm_scr, uc_scr,
                  s_scr, ssw_scr, xin_scr, yc_scr, ynat_scr):
    tm = x_ref.shape[1]
    n_chunks = tm // RET_CHUNK
    rows = tm // SSM_CHUNK
    qkw = RET_HEADS * RET_QK_DIM
    vw = RET_HEADS * RET_V_DIM
    ssm_w = SSM_GROUPS * SSM_GROUP_CH
    d_model = x_ref.shape[2]

    @pl.when(pl.program_id(1) == 0)
    def _():
        rstate[...] = jnp.zeros_like(rstate)
        sv[...] = jnp.zeros_like(sv)
        sw[...] = jnp.zeros_like(sw)

    x = x_ref[0]
    u = (x * _rms_scale(x) * gpre_ref[...]).astype(BF16)

    ang = pos_ref[0] * invf_ref[...]
    cos2 = jnp.cos(ang)
    sin = jnp.sin(ang)
    lane = lax.broadcasted_iota(jnp.int32, (tm, LANES), 1)
    sin2 = jnp.where(lane < RET_QK_DIM // 2, -sin, sin)

    q = _dot(u, win_ref[:, 0:qkw])
    k = _dot(u, win_ref[:, qkw:2 * qkw])
    kscale = RET_QK_DIM ** -0.5
    for h in range(RET_HEADS):
        sl = slice(h * RET_QK_DIM, (h + 1) * RET_QK_DIM)
        qh = q[:, sl]
        kh = k[:, sl]
        q_scr[:, sl] = (qh * cos2 + pltpu.roll(qh, RET_QK_DIM // 2, 1) * sin2).astype(BF16)
        k_scr[:, sl] = (kh * cos2 + pltpu.roll(kh, RET_QK_DIM // 2, 1) * sin2) * kscale
    v_scr[...] = _dot(u, win_ref[:, 2 * qkw:2 * qkw + vw]).astype(BF16)
    g_ret = _dot(u, win_ref[:, 2 * qkw + vw:2 * qkw + 2 * vw])
    sg_scr[...] = (g_ret * jax.nn.sigmoid(g_ret)).astype(BF16)

    def ret_chunk(c, carry):
        r0 = pl.multiple_of(c * RET_CHUNK, RET_CHUNK)
        rs = pl.ds(r0, RET_CHUNK)
        for h in range(RET_HEADS):
            qs = slice(h * RET_QK_DIM, (h + 1) * RET_QK_DIM)
            vs = slice(h * RET_V_DIM, (h + 1) * RET_V_DIM)
            qh = q_scr[rs, qs]
            kf = k_scr[rs, qs]
            vh = v_scr[rs, vs]
            scores = _dot_nt(qh, kf.astype(BF16)) * dec_ref[h]
            inner = _dot(scores.astype(BF16), vh)
            st = rstate[h]
            cross = _dot(qh, st.astype(BF16)) * xi_ref[h]
            kz_t = (kf * zeta_ref[h]).T.astype(BF16)
            rstate[h] = cd_ref[h] * st + _dot(kz_t, vh)
            ret = inner + cross
            dev = ret - jnp.mean(ret, axis=-1, keepdims=True)
            rn = dev * lax.rsqrt(jnp.mean(dev * dev, axis=-1, keepdims=True) + EPS)
            gated_scr[rs, vs] = (rn * sg_scr[rs, vs].astype(F32)).astype(BF16)
        return carry

    lax.fori_loop(0, n_chunks, ret_chunk, 0)
    y_a = _dot(gated_scr[...], wret_ref[...])

    o0 = 2 * qkw + 2 * vw
    u_ssm = _dot(u, win_ref[:, o0:o0 + ssm_w])
    for qt in range(ssm_w // LANES):
        ussm_scr[qt] = u_ssm[:, qt * LANES:(qt + 1) * LANES]
    lane_block = lax.broadcasted_iota(jnp.int32, (rows, LANES), 1) // SSM_GROUP_CH
    halves = SSM_CHUNK // 8
    for qt in range(ssm_w // LANES):
        for half in range(halves):
            pieces = [ussm_scr[qt, pl.ds(8 * half + j, rows, stride=SSM_CHUNK), :] for j in range(8)]
            t = _transpose8(pieces, lane_block)
            for gl in range(GROUPS_PER_LANE_TILE):
                uc_scr[qt * GROUPS_PER_LANE_TILE + gl, :, half * LANES:(half + 1) * LANES] = t[gl].astype(BF16)

    def s_phase(g, carry):
        s_g = _dot(uc_scr[g], wst_ref[g])
        rs = pl.ds(pl.multiple_of(g * rows, rows), rows)
        s_scr[rs, :] = s_g
        ssw_scr[rs, :] = pltpu.roll(s_g, SSM_STATE, 1)
        return carry

    lax.fori_loop(0, SSM_GROUPS, s_phase, 0)

    lam_a = lama_ref[...]
    lam_b = lamb_ref[...]

    def rec(n, carry):
        v, w = carry
        rsel = pl.ds(n, SSM_GROUPS, stride=rows)
        xin_scr[rsel, :] = v
        v_new = lam_a * v + lam_b * w + s_scr[rsel, :]
        w_new = lam_a * w - lam_b * v + ssw_scr[rsel, :]
        return v_new, w_new

    v_fin, w_fin = lax.fori_loop(0, rows, rec, (sv[...], sw[...]))
    sv[...] = v_fin
    sw[...] = w_fin

    def y_phase(g, carry):
        rs = pl.ds(pl.multiple_of(g * rows, rows), rows)
        yc_scr[g] = _dot(uc_scr[g], m_ref[g]) + _dot(xin_scr[rs, :].astype(BF16), cout_ref[g])
        return carry

    lax.fori_loop(0, SSM_GROUPS, y_phase, 0)

    for qt in range(ssm_w // LANES):
        for half in range(halves):
            pieces = [yc_scr[qt * GROUPS_PER_LANE_TILE + gl, :, half * LANES:(half + 1) * LANES]
                      for gl in range(GROUPS_PER_LANE_TILE)]
            t = _transpose8(pieces, lane_block)
            for i in range(8):
                ynat_scr[qt, pl.ds(8 * half + i, rows, stride=SSM_CHUNK), :] = t[i]

    z = jnp.concatenate([jax.nn.gelu(ynat_scr[qt]).astype(BF16) for qt in range(ssm_w // LANES)], axis=1)
    glu = _dot(z, wglu_ref[...])
    y_b = glu[:, :d_model] * jax.nn.sigmoid(glu[:, d_model:])

    o1 = o0 + ssm_w
    gate_a = jax.nn.sigmoid(_dot(u, win_ref[:, o1:o1 + d_model]))
    gate_b = jax.nn.sigmoid(_dot(u, win_ref[:, o1 + d_model:o1 + 2 * d_model]))
    mixed = _dot((gate_a * y_a + gate_b * y_b).astype(BF16), wout_ref[...])
    o_ref[0] = x_ref[0] + mixed * _rms_scale(mixed) * gpost_ref[...]


def _ffn_kernel(h_ref, gpre_ref, wg_ref, wu_ref, wd_ref, gpost_ref, o_ref, *, hidden_chunks):
    h = h_ref[...]
    f_in = (h * _rms_scale(h) * gpre_ref[...]).astype(BF16)
    f = None
    for lo, hi in hidden_chunks:
        gate = _dot(f_in, wg_ref[:, lo:hi])
        up = _dot(f_in, wu_ref[:, lo:hi])
        part = _dot((gate * jax.nn.sigmoid(gate) * up).astype(BF16), wd_ref[lo:hi, :])
        f = part if f is None else f + part
    o_ref[...] = h_ref[...] + f * _rms_scale(f) * gpost_ref[...]


def _resident(shape):
    nd = len(shape)
    return pl.BlockSpec(shape, lambda *_: (0,) * nd, pipeline_mode=pl.Buffered(1))


def _retention_constants():
    c = RET_CHUNK
    log_g = jnp.log(1.0 - 2.0 ** (-5.0 - jnp.arange(RET_HEADS, dtype=F32)))
    idx = jnp.arange(c, dtype=F32)
    diff = idx[:, None] - idx[None, :]
    inner_decay = jnp.where(diff >= 0, jnp.exp(log_g[:, None, None] * jnp.maximum(diff, 0.0)), 0.0)
    zeta = jnp.exp(log_g[:, None] * (c - 1.0 - idx)[None, :])
    xi = jnp.exp(log_g[:, None] * (idx + 1.0)[None, :])
    chunk_decay = jnp.exp(log_g * c)
    zeta_full = jnp.broadcast_to(zeta[:, :, None], (RET_HEADS, c, RET_QK_DIM))
    xi_full = jnp.broadcast_to(xi[:, :, None], (RET_HEADS, c, RET_V_DIM))
    cd_full = jnp.broadcast_to(chunk_decay[:, None, None], (RET_HEADS, 1, RET_V_DIM))
    return inner_decay.astype(F32), xi_full.astype(F32), zeta_full.astype(F32), cd_full.astype(F32)


def _s5_chunk_operators(a_re, a_im, log_dt, b_re, b_im, c_re, c_im, d_skip):
    tc = SSM_CHUNK
    dt = jnp.exp(log_dt)[:, None]
    da_re = dt * a_re
    da_im = dt * a_im
    mag = jnp.exp(da_re)
    ab_re = mag * jnp.cos(da_im)
    ab_im = mag * jnp.sin(da_im)
    den = a_re * a_re + a_im * a_im
    num_re = ab_re - 1.0
    f_re = (num_re * a_re + ab_im * a_im) / den
    f_im = (ab_im * a_re - num_re * a_im) / den
    bb_re = f_re[:, :, None] * b_re - f_im[:, :, None] * b_im
    bb_im = f_re[:, :, None] * b_im + f_im[:, :, None] * b_re
    d = jnp.arange(tc + 1, dtype=F32)[:, None, None]
    pmag = jnp.exp(d * da_re[None])
    pw_re = pmag * jnp.cos(d * da_im[None])
    pw_im = pmag * jnp.sin(d * da_im[None])
    lb_re = pw_re[:, :, :, None] * bb_re[None] - pw_im[:, :, :, None] * bb_im[None]
    lb_im = pw_re[:, :, :, None] * bb_im[None] + pw_im[:, :, :, None] * bb_re[None]
    hp = lax.Precision.HIGHEST
    kern = (jnp.einsum('gkp,dgpc->dgkc', c_re, lb_re[:tc], precision=hp)
            - jnp.einsum('gkp,dgpc->dgkc', c_im, lb_im[:tc], precision=hp))
    i_idx = jnp.arange(tc)
    lag = i_idx[None, :] - i_idx[:, None]
    m = jnp.where((lag >= 0)[:, :, None, None, None], kern[jnp.clip(lag, 0, tc - 1)], 0.0)
    skip = d_skip.reshape(SSM_GROUPS, SSM_GROUP_CH)
    eye_t = jnp.eye(tc, dtype=F32)
    eye_c = jnp.eye(SSM_GROUP_CH, dtype=F32)
    m = m + eye_t[:, :, None, None, None] * (skip[:, :, None] * eye_c[None])[None, None]
    m = jnp.transpose(m, (2, 0, 4, 1, 3)).reshape(SSM_GROUPS, SSM_CHUNK_W, SSM_CHUNK_W)
    wst = jnp.concatenate([lb_re[:tc][::-1], lb_im[:tc][::-1]], axis=2)
    wst = jnp.transpose(wst, (1, 0, 3, 2)).reshape(SSM_GROUPS, SSM_CHUNK_W, 2 * SSM_STATE)
    p1_re = pw_re[1:]
    p1_im = pw_im[1:]
    co_re = c_re[None] * p1_re[:, :, None, :] - c_im[None] * p1_im[:, :, None, :]
    co_im = -c_re[None] * p1_im[:, :, None, :] - c_im[None] * p1_re[:, :, None, :]
    cout = jnp.concatenate([co_re, co_im], axis=3)
    cout = jnp.transpose(cout, (1, 3, 0, 2)).reshape(SSM_GROUPS, 2 * SSM_STATE, SSM_CHUNK_W)
    lam_a = jnp.concatenate([pw_re[tc], pw_re[tc]], axis=1)
    lam_b = jnp.concatenate([-pw_im[tc], pw_im[tc]], axis=1)
    return m.astype(BF16), wst.astype(BF16), cout.astype(BF16), lam_a.astype(F32), lam_b.astype(F32)


def _mixer_call(x, pos_f, invf, gpre, win, ret_consts, wret, s5_ops, wglu, wout, gpost):
    bsz, seq, d_model = x.shape
    tm = min(MIXER_TILE, seq)
    assert seq % tm == 0 and tm % RET_CHUNK == 0
    rows = tm // SSM_CHUNK
    inner_decay, xi_full, zeta_full, cd_full = ret_consts
    m, wst, cout, lam_a, lam_b = s5_ops
    qkw = RET_HEADS * RET_QK_DIM
    vw = RET_HEADS * RET_V_DIM
    ssm_w = SSM_GROUPS * SSM_GROUP_CH
    consts = [invf, gpre, win, inner_decay, xi_full, zeta_full, cd_full, wret, m, wst, cout,
              lam_a, lam_b, wglu, wout, gpost]
    in_specs = [pl.BlockSpec((1, tm, d_model), lambda b, t: (b, t, 0)),
                pl.BlockSpec((1, tm, 1), lambda b, t: (b, t, 0))]
    in_specs += [_resident(c.shape) for c in consts]
    scratch = [
        pltpu.VMEM((RET_HEADS, RET_QK_DIM, RET_V_DIM), F32),
        pltpu.VMEM((SSM_GROUPS, 2 * SSM_STATE), F32),
        pltpu.VMEM((SSM_GROUPS, 2 * SSM_STATE), F32),
        pltpu.VMEM((tm, qkw), BF16),
        pltpu.VMEM((tm, qkw), F32),
        pltpu.VMEM((tm, vw), BF16),
        pltpu.VMEM((tm, vw), BF16),
        pltpu.VMEM((tm, vw), BF16),
        pltpu.VMEM((ssm_w // LANES, tm, LANES), F32),
        pltpu.VMEM((SSM_GROUPS, rows, SSM_CHUNK_W), BF16),
        pltpu.VMEM((SSM_GROUPS * rows, 2 * SSM_STATE), F32),
        pltpu.VMEM((SSM_GROUPS * rows, 2 * SSM_STATE), F32),
        pltpu.VMEM((SSM_GROUPS * rows, 2 * SSM_STATE), F32),
        pltpu.VMEM((SSM_GROUPS, rows, SSM_CHUNK_W), F32),
        pltpu.VMEM((ssm_w // LANES, tm, LANES), F32),
    ]
    return pl.pallas_call(
        _mixer_kernel,
        grid=(bsz, seq // tm),
        in_specs=in_specs,
        out_specs=pl.BlockSpec((1, tm, d_model), lambda b, t: (b, t, 0)),
        out_shape=jax.ShapeDtypeStruct(x.shape, F32),
        scratch_shapes=scratch,
        compiler_params=pltpu.CompilerParams(
            dimension_semantics=("arbitrary", "arbitrary"),
            vmem_limit_bytes=VMEM_LIMIT_BYTES),
        name="mixer",
    )(x, pos_f, *consts)


def _ffn_call(h, gpre, wg, wu, wd, gpost):
    n_tok, d_model = h.shape
    hidden = wg.shape[1]
    tm = min(FFN_TILE, n_tok)
    assert n_tok % tm == 0
    step = 1024
    hidden_chunks = tuple((lo, min(lo + step, hidden)) for lo in range(0, hidden, step))
    consts = [gpre, wg, wu, wd, gpost]
    return pl.pallas_call(
        functools.partial(_ffn_kernel, hidden_chunks=hidden_chunks),
        grid=(n_tok // tm,),
        in_specs=[pl.BlockSpec((tm, d_model), lambda i: (i, 0))] + [_resident(c.shape) for c in consts],
        out_specs=pl.BlockSpec((tm, d_model), lambda i: (i, 0)),
        out_shape=jax.ShapeDtypeStruct(h.shape, F32),
        compiler_params=pltpu.CompilerParams(
            dimension_semantics=("arbitrary",),
            vmem_limit_bytes=VMEM_LIMIT_BYTES),
        name="ffn",
    )(h, *consts)


def kernel(x, positions, mix_pre_norm, w_in, ssm_a_re, ssm_a_im, ssm_log_dt, ssm_b_re, ssm_b_im, ssm_c_re, ssm_c_im, ssm_d, w_glu_val, w_glu_gate, w_ret_up, w_out, mix_post_norm, ffn_pre_norm, w_ffn_gate, w_ffn_up, w_ffn_down, ffn_post_norm):
    bsz, seq, d_model = x.shape
    depth = w_in.shape[0]
    pos_f = positions.astype(F32).reshape(bsz, seq, 1)
    inv_freq = ROPE_BASE ** (-jnp.arange(0, RET_QK_DIM, 2, dtype=F32) / RET_QK_DIM)
    invf = jnp.concatenate([inv_freq, inv_freq]).reshape(1, RET_QK_DIM)
    ret_consts = _retention_constants()
    h = x
    for layer in range(depth):
        s5_ops = _s5_chunk_operators(ssm_a_re[layer], ssm_a_im[layer], ssm_log_dt[layer], ssm_b_re[layer],
                                     ssm_b_im[layer], ssm_c_re[layer], ssm_c_im[layer], ssm_d[layer])
        wglu = jnp.concatenate([w_glu_val[layer], w_glu_gate[layer]], axis=1).astype(BF16)
        h = _mixer_call(h, pos_f, invf, mix_pre_norm[layer].reshape(1, d_model), w_in[layer].astype(BF16),
                        ret_consts, w_ret_up[layer].astype(BF16), s5_ops, wglu, w_out[layer].astype(BF16),
                        mix_post_norm[layer].reshape(1, d_model))
        h = _ffn_call(h.reshape(bsz * seq, d_model), ffn_pre_norm[layer].reshape(1, d_model),
                      w_ffn_gate[layer].astype(BF16), w_ffn_up[layer].astype(BF16),
                      w_ffn_down[layer].astype(BF16), ffn_post_norm[layer].reshape(1, d_model)
                      ).reshape(bsz, seq, d_model)
    return h
```

```python
import functools
import math

import jax
import jax.numpy as jnp
from jax import lax
from jax.experimental import pallas as pl
from jax.experimental.pallas import tpu as pltpu

F32 = jnp.float32
BF16 = jnp.bfloat16

EPS = 1e-6
ROPE_BASE = 10000.0
RET_HEADS = 4
RET_QK_DIM = 128
RET_V_DIM = 256
RET_CHUNK = 128
SSM_GROUP_CH = 16
SSM_GROUPS = 32
SSM_STATE = 64
SSM_CHUNK = 16
SSM_CHUNK_W = SSM_CHUNK * SSM_GROUP_CH
LANES = 128
GROUPS_PER_LANE_TILE = LANES // SSM_GROUP_CH
VMEM_LIMIT_BYTES = 58 * 1024 * 1024

MIXER_TILE = 512
FFN_TILE = 512


def _dot(a, b):
    return jnp.dot(a, b, preferred_element_type=F32)


def _dot_nt(a, b):
    return lax.dot_general(a, b, (((1,), (1,)), ((), ())), preferred_element_type=F32)


def _rms_scale(x):
    return lax.rsqrt(jnp.mean(x * x, axis=-1, keepdims=True) + EPS)


def _transpose8(pieces, lane_block):
    a = list(pieces)
    s = 4
    while s >= 1:
        keep = (lane_block & s) == 0
        shift = SSM_GROUP_CH * s
        for i in range(8):
            if i & s:
                continue
            lo, hi = a[i], a[i + s]
            a[i] = jnp.where(keep, lo, pltpu.roll(hi, shift, 1))
            a[i + s] = jnp.where(keep, pltpu.roll(lo, LANES - shift, 1), hi)
        s //= 2
    return a


def _mixer_kernel(x_ref, pos_ref, invf_ref, gpre_ref, win_ref, dec_ref, xi_ref, zeta_ref, cd_ref,
                  wret_ref, m_ref, wst_ref, cout_ref, lama_ref, lamb_ref, wglu_ref, wout_ref,
                  gpost_ref, o_ref,
                  rstate, sv, sw, q_scr, k_scr, v_scr, sg_scr, gated_scr, ussm_scr, uc_scr,
                  s_scr, ssw_scr, xin_scr, yc_scr, ynat_scr):
    tm = x_ref.shape[1]
    n_chunks = tm // RET_CHUNK
    rows = tm // SSM_CHUNK
    qkw = RET_HEADS * RET_QK_DIM
    vw = RET_HEADS * RET_V_DIM
    ssm_w = SSM_GROUPS * SSM_GROUP_CH
    d_model = x_ref.shape[2]

    @pl.when(pl.program_id(1) == 0)
    def _():
        rstate[...] = jnp.zeros_like(rstate)
        sv[...] = jnp.zeros_like(sv)
        sw[...] = jnp.zeros_like(sw)

    x = x_ref[0]
    u = (x * _rms_scale(x) * gpre_ref[...]).astype(BF16)

    ang = pos_ref[0] * invf_ref[...]
    cos2 = jnp.cos(ang)
    sin = jnp.sin(ang)
    lane = lax.broadcasted_iota(jnp.int32, (tm, LANES), 1)
    sin2 = jnp.where(lane < RET_QK_DIM // 2, -sin, sin)

    q = _dot(u, win_ref[:, 0:qkw])
    k = _dot(u, win_ref[:, qkw:2 * qkw])
    kscale = RET_QK_DIM ** -0.5
    for h in range(RET_HEADS):
        sl = slice(h * RET_QK_DIM, (h + 1) * RET_QK_DIM)
        qh = q[:, sl]
        kh = k[:, sl]
        q_scr[:, sl] = (qh * cos2 + pltpu.roll(qh, RET_QK_DIM // 2, 1) * sin2).astype(BF16)
        k_scr[:, sl] = (kh * cos2 + pltpu.roll(kh, RET_QK_DIM // 2, 1) * sin2) * kscale
    v_scr[...] = _dot(u, win_ref[:, 2 * qkw:2 * qkw + vw]).astype(BF16)
    g_ret = _dot(u, win_ref[:, 2 * qkw + vw:2 * qkw + 2 * vw])
    sg_scr[...] = (g_ret * jax.nn.sigmoid(g_ret)).astype(BF16)

    def ret_chunk(c, carry):
        r0 = pl.multiple_of(c * RET_CHUNK, RET_CHUNK)
        rs = pl.ds(r0, RET_CHUNK)
        for h in range(RET_HEADS):
            qs = slice(h * RET_QK_DIM, (h + 1) * RET_QK_DIM)
            vs = slice(h * RET_V_DIM, (h + 1) * RET_V_DIM)
            qh = q_scr[rs, qs]
            kf = k_scr[rs, qs]
            vh = v_scr[rs, vs]
            scores = _dot_nt(qh, kf.astype(BF16)) * dec_ref[h]
            inner = _dot(scores.astype(BF16), vh)
            st = rstate[h]
            cross = _dot(qh, st.astype(BF16)) * xi_ref[h]
            kz_t = (kf * zeta_ref[h]).T.astype(BF16)
            rstate[h] = cd_ref[h] * st + _dot(kz_t, vh)
            ret = inner + cross
            dev = ret - jnp.mean(ret, axis=-1, keepdims=True)
            rn = dev * lax.rsqrt(jnp.mean(dev * dev, axis=-1, keepdims=True) + EPS)
            gated_scr[rs, vs] = (rn * sg_scr[rs, vs].astype(F32)).astype(BF16)
        return carry

    lax.fori_loop(0, n_chunks, ret_chunk, 0)
    y_a = _dot(gated_scr[...], wret_ref[...])

    o0 = 2 * qkw + 2 * vw
    u_ssm = _dot(u, win_ref[:, o0:o0 + ssm_w])
    for qt in range(ssm_w // LANES):
        ussm_scr[qt] = u_ssm[:, qt * LANES:(qt + 1) * LANES]
    lane_block = lax.broadcasted_iota(jnp.int32, (rows, LANES), 1) // SSM_GROUP_CH
    halves = SSM_CHUNK // 8
    for qt in range(ssm_w // LANES):
        for half in range(halves):
            pieces = [ussm_scr[qt, pl.ds(8 * half + j, rows, stride=SSM_CHUNK), :] for j in range(8)]
            t = _transpose8(pieces, lane_block)
            for gl in range(GROUPS_PER_LANE_TILE):
                uc_scr[qt * GROUPS_PER_LANE_TILE + gl, :, half * LANES:(half + 1) * LANES] = t[gl].astype(BF16)

    for g in range(SSM_GROUPS):
        s_g = _dot(uc_scr[g], wst_ref[g])
        s_scr[g * rows:(g + 1) * rows, :] = s_g
        ssw_scr[g * rows:(g + 1) * rows, :] = pltpu.roll(s_g, SSM_STATE, 1)

    lam_a = lama_ref[...]
    lam_b = lamb_ref[...]
    v = sv[...]
    w = sw[...]
    for n in range(rows):
        rsel = pl.ds(n, SSM_GROUPS, stride=rows)
        xin_scr[rsel, :] = v
        v, w = (lam_a * v + lam_b * w + s_scr[rsel, :],
                lam_a * w - lam_b * v + ssw_scr[rsel, :])
    sv[...] = v
    sw[...] = w

    for g in range(SSM_GROUPS):
        x_in = xin_scr[g * rows:(g + 1) * rows, :].astype(BF16)
        yc_scr[g] = _dot(uc_scr[g], m_ref[g]) + _dot(x_in, cout_ref[g])

    for qt in range(ssm_w // LANES):
        for half in range(halves):
            pieces = [yc_scr[qt * GROUPS_PER_LANE_TILE + gl, :, half * LANES:(half + 1) * LANES]
                      for gl in range(GROUPS_PER_LANE_TILE)]
            t = _transpose8(pieces, lane_block)
            for i in range(8):
                ynat_scr[qt, pl.ds(8 * half + i, rows, stride=SSM_CHUNK), :] = t[i]

    z = jnp.concatenate([jax.nn.gelu(ynat_scr[qt]).astype(BF16) for qt in range(ssm_w // LANES)], axis=1)
    glu = _dot(z, wglu_ref[...])
    y_b = glu[:, :d_model] * jax.nn.sigmoid(glu[:, d_model:])

    o1 = o0 + ssm_w
    gate_a = jax.nn.sigmoid(_dot(u, win_ref[:, o1:o1 + d_model]))
    gate_b = jax.nn.sigmoid(_dot(u, win_ref[:, o1 + d_model:o1 + 2 * d_model]))
    mixed = _dot((gate_a * y_a + gate_b * y_b).astype(BF16), wout_ref[...])
    o_ref[0] = x_ref[0] + mixed * _rms_scale(mixed) * gpost_ref[...]


def _ffn_kernel(h_ref, gpre_ref, wg_ref, wu_ref, wd_ref, gpost_ref, o_ref, *, hidden_chunks):
    h = h_ref[...]
    f_in = (h * _rms_scale(h) * gpre_ref[...]).astype(BF16)
    f = None
    for lo, hi in hidden_chunks:
        gate = _dot(f_in, wg_ref[:, lo:hi])
        up = _dot(f_in, wu_ref[:, lo:hi])
        part = _dot((gate * jax.nn.sigmoid(gate) * up).astype(BF16), wd_ref[lo:hi, :])
        f = part if f is None else f + part
    o_ref[...] = h_ref[...] + f * _rms_scale(f) * gpost_ref[...]


def _resident(shape):
    nd = len(shape)
    return pl.BlockSpec(shape, lambda *_: (0,) * nd, pipeline_mode=pl.Buffered(1))


def _retention_constants():
    c = RET_CHUNK
    log_g = jnp.log(1.0 - 2.0 ** (-5.0 - jnp.arange(RET_HEADS, dtype=F32)))
    idx = jnp.arange(c, dtype=F32)
    diff = idx[:, None] - idx[None, :]
    inner_decay = jnp.where(diff >= 0, jnp.exp(log_g[:, None, None] * jnp.maximum(diff, 0.0)), 0.0)
    zeta = jnp.exp(log_g[:, None] * (c - 1.0 - idx)[None, :])
    xi = jnp.exp(log_g[:, None] * (idx + 1.0)[None, :])
    chunk_decay = jnp.exp(log_g * c)
    zeta_full = jnp.broadcast_to(zeta[:, :, None], (RET_HEADS, c, RET_QK_DIM))
    xi_full = jnp.broadcast_to(xi[:, :, None], (RET_HEADS, c, RET_V_DIM))
    cd_full = jnp.broadcast_to(chunk_decay[:, None, None], (RET_HEADS, 1, RET_V_DIM))
    return inner_decay.astype(F32), xi_full.astype(F32), zeta_full.astype(F32), cd_full.astype(F32)


def _s5_chunk_operators(a_re, a_im, log_dt, b_re, b_im, c_re, c_im, d_skip):
    tc = SSM_CHUNK
    dt = jnp.exp(log_dt)[:, None]
    da_re = dt * a_re
    da_im = dt * a_im
    mag = jnp.exp(da_re)
    ab_re = mag * jnp.cos(da_im)
    ab_im = mag * jnp.sin(da_im)
    den = a_re * a_re + a_im * a_im
    num_re = ab_re - 1.0
    f_re = (num_re * a_re + ab_im * a_im) / den
    f_im = (ab_im * a_re - num_re * a_im) / den
    bb_re = f_re[:, :, None] * b_re - f_im[:, :, None] * b_im
    bb_im = f_re[:, :, None] * b_im + f_im[:, :, None] * b_re
    d = jnp.arange(tc + 1, dtype=F32)[:, None, None]
    pmag = jnp.exp(d * da_re[None])
    pw_re = pmag * jnp.cos(d * da_im[None])
    pw_im = pmag * jnp.sin(d * da_im[None])
    lb_re = pw_re[:, :, :, None] * bb_re[None] - pw_im[:, :, :, None] * bb_im[None]
    lb_im = pw_re[:, :, :, None] * bb_im[None] + pw_im[:, :, :, None] * bb_re[None]
    hp = lax.Precision.HIGHEST
    kern = (jnp.einsum('gkp,dgpc->dgkc', c_re, lb_re[:tc], precision=hp)
            - jnp.einsum('gkp,dgpc->dgkc', c_im, lb_im[:tc], precision=hp))
    i_idx = jnp.arange(tc)
    lag = i_idx[None, :] - i_idx[:, None]
    m = jnp.where((lag >= 0)[:, :, None, None, None], kern[jnp.clip(lag, 0, tc - 1)], 0.0)
    skip = d_skip.reshape(SSM_GROUPS, SSM_GROUP_CH)
    eye_t = jnp.eye(tc, dtype=F32)
    eye_c = jnp.eye(SSM_GROUP_CH, dtype=F32)
    m = m + eye_t[:, :, None, None, None] * (skip[:, :, None] * eye_c[None])[None, None]
    m = jnp.transpose(m, (2, 0, 4, 1, 3)).reshape(SSM_GROUPS, SSM_CHUNK_W, SSM_CHUNK_W)
    wst = jnp.concatenate([lb_re[:tc][::-1], lb_im[:tc][::-1]], axis=2)
    wst = jnp.transpose(wst, (1, 0, 3, 2)).reshape(SSM_GROUPS, SSM_CHUNK_W, 2 * SSM_STATE)
    p1_re = pw_re[1:]
    p1_im = pw_im[1:]
    co_re = c_re[None] * p1_re[:, :, None, :] - c_im[None] * p1_im[:, :, None, :]
    co_im = -c_re[None] * p1_im[:, :, None, :] - c_im[None] * p1_re[:, :, None, :]
    cout = jnp.concatenate([co_re, co_im], axis=3)
    cout = jnp.transpose(cout, (1, 3, 0, 2)).reshape(SSM_GROUPS, 2 * SSM_STATE, SSM_CHUNK_W)
    lam_a = jnp.concatenate([pw_re[tc], pw_re[tc]], axis=1)
    lam_b = jnp.concatenate([-pw_im[tc], pw_im[tc]], axis=1)
    return m.astype(BF16), wst.astype(BF16), cout.astype(BF16), lam_a.astype(F32), lam_b.astype(F32)


def _mixer_call(x, pos_f, invf, gpre, win, ret_consts, wret, s5_ops, wglu, wout, gpost):
    bsz, seq, d_model = x.shape
    tm = min(MIXER_TILE, seq)
    assert seq % tm == 0 and tm % RET_CHUNK == 0
    rows = tm // SSM_CHUNK
    inner_decay, xi_full, zeta_full, cd_full = ret_consts
    m, wst, cout, lam_a, lam_b = s5_ops
    qkw = RET_HEADS * RET_QK_DIM
    vw = RET_HEADS * RET_V_DIM
    ssm_w = SSM_GROUPS * SSM_GROUP_CH
    consts = [invf, gpre, win, inner_decay, xi_full, zeta_full, cd_full, wret, m, wst, cout,
              lam_a, lam_b, wglu, wout, gpost]
    in_specs = [pl.BlockSpec((1, tm, d_model), lambda b, t: (b, t, 0)),
                pl.BlockSpec((1, tm, 1), lambda b, t: (b, t, 0))]
    in_specs += [_resident(c.shape) for c in consts]
    scratch = [
        pltpu.VMEM((RET_HEADS, RET_QK_DIM, RET_V_DIM), F32),
        pltpu.VMEM((SSM_GROUPS, 2 * SSM_STATE), F32),
        pltpu.VMEM((SSM_GROUPS, 2 * SSM_STATE), F32),
        pltpu.VMEM((tm, qkw), BF16),
        pltpu.VMEM((tm, qkw), F32),
        pltpu.VMEM((tm, vw), BF16),
        pltpu.VMEM((tm, vw), BF16),
        pltpu.VMEM((tm, vw), BF16),
        pltpu.VMEM((ssm_w // LANES, tm, LANES), F32),
        pltpu.VMEM((SSM_GROUPS, rows, SSM_CHUNK_W), BF16),
        pltpu.VMEM((SSM_GROUPS * rows, 2 * SSM_STATE), F32),
        pltpu.VMEM((SSM_GROUPS * rows, 2 * SSM_STATE), F32),
        pltpu.VMEM((SSM_GROUPS * rows, 2 * SSM_STATE), F32),
        pltpu.VMEM((SSM_GROUPS, rows, SSM_CHUNK_W), F32),
        pltpu.VMEM((ssm_w // LANES, tm, LANES), F32),
    ]
    return pl.pallas_call(
        _mixer_kernel,
        grid=(bsz, seq // tm),
        in_specs=in_specs,
        out_specs=pl.BlockSpec((1, tm, d_model), lambda b, t: (b, t, 0)),
        out_shape=jax.ShapeDtypeStruct(x.shape, F32),
        scratch_shapes=scratch,
        compiler_params=pltpu.CompilerParams(
            dimension_semantics=("arbitrary", "arbitrary"),
            vmem_limit_bytes=VMEM_LIMIT_BYTES),
        name="mixer",
    )(x, pos_f, *consts)


def _ffn_call(h, gpre, wg, wu, wd, gpost):
    n_tok, d_model = h.shape
    hidden = wg.shape[1]
    tm = min(FFN_TILE, n_tok)
    assert n_tok % tm == 0
    step = 1024
    hidden_chunks = tuple((lo, min(lo + step, hidden)) for lo in range(0, hidden, step))
    consts = [gpre, wg, wu, wd, gpost]
    return pl.pallas_call(
        functools.partial(_ffn_kernel, hidden_chunks=hidden_chunks),
        grid=(n_tok // tm,),
        in_specs=[pl.BlockSpec((tm, d_model), lambda i: (i, 0))] + [_resident(c.shape) for c in consts],
        out_specs=pl.BlockSpec((tm, d_model), lambda i: (i, 0)),
        out_shape=jax.ShapeDtypeStruct(h.shape, F32),
        compiler_params=pltpu.CompilerParams(
            dimension_semantics=("arbitrary",),
            vmem_limit_bytes=VMEM_LIMIT_BYTES),
        name="ffn",
    )(h, *consts)


def kernel(x, positions, mix_pre_norm, w_in, ssm_a_re, ssm_a_im, ssm_log_dt, ssm_b_re, ssm_b_im, ssm_c_re, ssm_c_im, ssm_d, w_glu_val, w_glu_gate, w_ret_up, w_out, mix_post_norm, ffn_pre_norm, w_ffn_gate, w_ffn_up, w_ffn_down, ffn_post_norm):
    bsz, seq, d_model = x.shape
    depth = w_in.shape[0]
    pos_f = positions.astype(F32).reshape(bsz, seq, 1)
    inv_freq = ROPE_BASE ** (-jnp.arange(0, RET_QK_DIM, 2, dtype=F32) / RET_QK_DIM)
    invf = jnp.concatenate([inv_freq, inv_freq]).reshape(1, RET_QK_DIM)
    ret_consts = _retention_constants()
    h = x
    for layer in range(depth):
        s5_ops = _s5_chunk_operators(ssm_a_re[layer], ssm_a_im[layer], ssm_log_dt[layer], ssm_b_re[layer],
                                     ssm_b_im[layer], ssm_c_re[layer], ssm_c_im[layer], ssm_d[layer])
        wglu = jnp.concatenate([w_glu_val[layer], w_glu_gate[layer]], axis=1).astype(BF16)
        h = _mixer_call(h, pos_f, invf, mix_pre_norm[layer].reshape(1, d_model), w_in[layer].astype(BF16),
                        ret_consts, w_ret_up[layer].astype(BF16), s5_ops, wglu, w_out[layer].astype(BF16),
                        mix_post_norm[layer].reshape(1, d_model))
        h = _ffn_call(h.reshape(bsz * seq, d_model), ffn_pre_norm[layer].reshape(1, d_model),
                      w_ffn_gate[layer].astype(BF16), w_ffn_up[layer].astype(BF16),
                      w_ffn_down[layer].astype(BF16), ffn_post_norm[layer].reshape(1, d_model)
                      ).reshape(bsz, seq, d_model)
    return h
```

```python
import functools

import jax
import jax.numpy as jnp
from jax import lax
from jax.experimental import pallas as pl
from jax.experimental.pallas import tpu as pltpu

F32 = jnp.float32
BF16 = jnp.bfloat16

EPS = 1e-6
ROPE_BASE = 10000.0
RET_HEADS = 4
RET_QK_DIM = 128
RET_V_DIM = 256
RET_CHUNK = 128
SSM_GROUP_CH = 16
SSM_GROUPS = 32
SSM_STATE = 64
SSM_CHUNK = 16
SSM_CHUNK_W = SSM_CHUNK * SSM_GROUP_CH
LANES = 128
GROUPS_PER_LANE_TILE = LANES // SSM_GROUP_CH
VMEM_LIMIT_BYTES = 58 * 1024 * 1024

MIXER_TILE = 512
FFN_TILE = 512


def _dot(a, b):
    return jnp.dot(a, b, preferred_element_type=F32)


def _dot_nt(a, b):
    return lax.dot_general(a, b, (((1,), (1,)), ((), ())), preferred_element_type=F32)


def _sigmoid(x):
    return 0.5 * jnp.tanh(0.5 * x) + 0.5


def _rms_scale(x):
    return lax.rsqrt(jnp.mean(x * x, axis=-1, keepdims=True) + EPS)


def _transpose8(pieces, lane_block):
    a = list(pieces)
    s = 4
    while s >= 1:
        keep = (lane_block & s) == 0
        shift = SSM_GROUP_CH * s
        for i in range(8):
            if i & s:
                continue
            lo, hi = a[i], a[i + s]
            a[i] = jnp.where(keep, lo, pltpu.roll(hi, shift, 1))
            a[i + s] = jnp.where(keep, pltpu.roll(lo, LANES - shift, 1), hi)
        s //= 2
    return a


def _state_row_pitch(rows):
    assert rows % 8 == 0
    return rows + 4


def _mixer_kernel(x_ref, pos_ref, invf_ref, gpre_ref, win_ref, dec_ref, xi_ref, zeta_ref, cd_ref,
                  wret_ref, m_ref, wst_ref, cout_ref, lama_ref, lamb_ref, wglu_ref, wout_ref,
                  gpost_ref, o_ref,
                  rstate, sv, sw, u_scr, q_scr, k_scr, v_scr, sg_scr, gated_scr, ussm_scr, uc_scr,
                  s_scr, ssw_scr, xin_scr, yc_scr, ynat_scr):
    tm = x_ref.shape[1]
    n_chunks = tm // RET_CHUNK
    rows = tm // SSM_CHUNK
    pitch = _state_row_pitch(rows)
    qkw = RET_HEADS * RET_QK_DIM
    vw = RET_HEADS * RET_V_DIM
    ssm_w = SSM_GROUPS * SSM_GROUP_CH
    lane_tiles = ssm_w // LANES
    halves = SSM_CHUNK // 8
    d_model = x_ref.shape[2]
    hq = RET_QK_DIM // 2
    o_v = 2 * qkw
    o_gr = o_v + vw
    o_ssm = o_gr + vw
    o_ga = o_ssm + ssm_w
    o_gb = o_ga + d_model

    @pl.when(pl.program_id(1) == 0)
    def _():
        rstate[...] = jnp.zeros_like(rstate)
        sv[...] = jnp.zeros_like(sv)
        sw[...] = jnp.zeros_like(sw)

    def proj(lo, width):
        return _dot(u_scr[...], win_ref[:, lo:lo + width])

    lane_block = lax.broadcasted_iota(jnp.int32, (rows, LANES), 1) // SSM_GROUP_CH


    x = x_ref[0]
    u_scr[...] = (x * _rms_scale(x) * gpre_ref[...]).astype(BF16)

    u_ssm = proj(o_ssm, ssm_w)
    for qt in range(lane_tiles):
        ussm_scr[qt] = u_ssm[:, qt * LANES:(qt + 1) * LANES]

    q = proj(0, qkw)
    k = proj(qkw, qkw)

    for qt in range(lane_tiles):
        for half in range(halves):
            pieces = [ussm_scr[qt, pl.ds(8 * half + j, rows, stride=SSM_CHUNK), :] for j in range(8)]
            t = _transpose8(pieces, lane_block)
            for gl in range(GROUPS_PER_LANE_TILE):
                uc_scr[qt * GROUPS_PER_LANE_TILE + gl, :, half * LANES:(half + 1) * LANES] = t[gl].astype(BF16)

    half_tm = tm // 2
    first = lax.broadcasted_iota(jnp.int32, (half_tm, LANES), 1) < hq
    invf = invf_ref[...]
    ang = jnp.where(first, pos_ref[0, 0:half_tm, :] * invf, pos_ref[0, half_tm:tm, :] * invf)
    cs = jnp.cos(ang)
    sn = jnp.sin(ang)
    cs_sw = pltpu.roll(cs, hq, 1)
    sn_sw = pltpu.roll(sn, hq, 1)
    cos2 = jnp.concatenate([jnp.where(first, cs, cs_sw), jnp.where(first, cs_sw, cs)], axis=0)
    sin2 = jnp.concatenate([jnp.where(first, -sn, sn_sw), jnp.where(first, -sn_sw, sn)], axis=0)
    kscale = RET_QK_DIM ** -0.5
    for h in range(RET_HEADS):
        sl = slice(h * RET_QK_DIM, (h + 1) * RET_QK_DIM)
        qh = q[:, sl]
        kh = k[:, sl]
        q_scr[:, sl] = (qh * cos2 + pltpu.roll(qh, hq, 1) * sin2).astype(BF16)
        k_scr[:, sl] = (kh * cos2 + pltpu.roll(kh, hq, 1) * sin2) * kscale

    for g in range(SSM_GROUPS):
        s_g = _dot(uc_scr[g], wst_ref[g])
        s_scr[g * pitch:g * pitch + rows, :] = s_g
        ssw_scr[g * pitch:g * pitch + rows, :] = pltpu.roll(s_g, SSM_STATE, 1)

    v_scr[...] = proj(o_v, vw).astype(BF16)
    g_ret = proj(o_gr, vw)
    sg_scr[...] = (g_ret * _sigmoid(g_ret)).astype(BF16)

    gate_a = _sigmoid(proj(o_ga, d_model))
    gate_b = _sigmoid(proj(o_gb, d_model))

    lam_a = lama_ref[...]
    lam_b = lamb_ref[...]
    v = sv[...]
    w = sw[...]
    for n in range(rows):
        rsel = pl.ds(n, SSM_GROUPS, stride=pitch)
        xin_scr[rsel, :] = v
        v, w = (lam_a * v + lam_b * w + s_scr[rsel, :],
                lam_a * w - lam_b * v + ssw_scr[rsel, :])
    sv[...] = v
    sw[...] = w

    for g in range(SSM_GROUPS):
        x_in = xin_scr[g * pitch:g * pitch + rows, :].astype(BF16)
        yc_scr[g] = _dot(uc_scr[g], m_ref[g]) + _dot(x_in, cout_ref[g])

    for c in range(n_chunks):
        rs = slice(c * RET_CHUNK, (c + 1) * RET_CHUNK)
        qsl = [slice(h * RET_QK_DIM, (h + 1) * RET_QK_DIM) for h in range(RET_HEADS)]
        vsl = [slice(h * RET_V_DIM, (h + 1) * RET_V_DIM) for h in range(RET_HEADS)]
        qh = [q_scr[rs, qsl[h]] for h in range(RET_HEADS)]
        kf = [k_scr[rs, qsl[h]] for h in range(RET_HEADS)]
        vh = [v_scr[rs, vsl[h]] for h in range(RET_HEADS)]
        scores = [_dot_nt(qh[h], kf[h].astype(BF16)) * dec_ref[h] for h in range(RET_HEADS)]
        st = [rstate[h] for h in range(RET_HEADS)]
        cross = [_dot(qh[h], st[h].astype(BF16)) * xi_ref[h] for h in range(RET_HEADS)]
        for h in range(RET_HEADS):
            kz_t = (kf[h] * zeta_ref[h]).T.astype(BF16)
            rstate[h] = cd_ref[h] * st[h] + _dot(kz_t, vh[h])
        for h in range(RET_HEADS):
            ret = _dot(scores[h].astype(BF16), vh[h]) + cross[h]
            dev = ret - jnp.mean(ret, axis=-1, keepdims=True)
            rn = dev * lax.rsqrt(jnp.mean(dev * dev, axis=-1, keepdims=True) + EPS)
            gated_scr[rs, vsl[h]] = (rn * sg_scr[rs, vsl[h]].astype(F32)).astype(BF16)

    for qt in range(lane_tiles):
        for half in range(halves):
            pieces = [yc_scr[qt * GROUPS_PER_LANE_TILE + gl, :, half * LANES:(half + 1) * LANES]
                      for gl in range(GROUPS_PER_LANE_TILE)]
            t = _transpose8(pieces, lane_block)
            for i in range(8):
                ynat_scr[qt, pl.ds(8 * half + i, rows, stride=SSM_CHUNK), :] = t[i]
    z = jnp.concatenate([jax.nn.gelu(ynat_scr[qt]).astype(BF16) for qt in range(lane_tiles)], axis=1)

    glu = _dot(z, wglu_ref[...])
    y_b = glu[:, :d_model] * _sigmoid(glu[:, d_model:])
    y_a = _dot(gated_scr[...], wret_ref[...])

    mixed = _dot((gate_a * y_a + gate_b * y_b).astype(BF16), wout_ref[...])
    o_ref[0] = x_ref[0] + mixed * _rms_scale(mixed) * gpost_ref[...]


def _ffn_kernel(h_ref, gpre_ref, wg_ref, wu_ref, wd_ref, gpost_ref, o_ref, *, hidden_chunks):
    h = h_ref[...]
    f_in = (h * _rms_scale(h) * gpre_ref[...]).astype(BF16)
    f = None
    for lo, hi in hidden_chunks:
        gate = _dot(f_in, wg_ref[:, lo:hi])
        up = _dot(f_in, wu_ref[:, lo:hi])
        part = _dot((gate * _sigmoid(gate) * up).astype(BF16), wd_ref[lo:hi, :])
        f = part if f is None else f + part
    o_ref[...] = h_ref[...] + f * _rms_scale(f) * gpost_ref[...]


def _resident(shape):
    nd = len(shape)
    return pl.BlockSpec(shape, lambda *_: (0,) * nd, pipeline_mode=pl.Buffered(1))


def _retention_constants():
    c = RET_CHUNK
    log_g = jnp.log(1.0 - 2.0 ** (-5.0 - jnp.arange(RET_HEADS, dtype=F32)))
    idx = jnp.arange(c, dtype=F32)
    diff = idx[:, None] - idx[None, :]
    inner_decay = jnp.where(diff >= 0, jnp.exp(log_g[:, None, None] * jnp.maximum(diff, 0.0)), 0.0)
    zeta = jnp.exp(log_g[:, None] * (c - 1.0 - idx)[None, :])
    xi = jnp.exp(log_g[:, None] * (idx + 1.0)[None, :])
    chunk_decay = jnp.exp(log_g * c)
    zeta_full = jnp.broadcast_to(zeta[:, :, None], (RET_HEADS, c, RET_QK_DIM))
    xi_full = jnp.broadcast_to(xi[:, :, None], (RET_HEADS, c, RET_V_DIM))
    cd_full = jnp.broadcast_to(chunk_decay[:, None, None], (RET_HEADS, 1, RET_V_DIM))
    return inner_decay.astype(F32), xi_full.astype(F32), zeta_full.astype(F32), cd_full.astype(F32)


def _s5_chunk_operators(a_re, a_im, log_dt, b_re, b_im, c_re, c_im, d_skip):
    tc = SSM_CHUNK
    dt = jnp.exp(log_dt)[:, None]
    da_re = dt * a_re
    da_im = dt * a_im
    mag = jnp.exp(da_re)
    ab_re = mag * jnp.cos(da_im)
    ab_im = mag * jnp.sin(da_im)
    den = a_re * a_re + a_im * a_im
    num_re = ab_re - 1.0
    f_re = (num_re * a_re + ab_im * a_im) / den
    f_im = (ab_im * a_re - num_re * a_im) / den
    bb_re = f_re[:, :, None] * b_re - f_im[:, :, None] * b_im
    bb_im = f_re[:, :, None] * b_im + f_im[:, :, None] * b_re
    d = jnp.arange(tc + 1, dtype=F32)[:, None, None]
    pmag = jnp.exp(d * da_re[None])
    pw_re = pmag * jnp.cos(d * da_im[None])
    pw_im = pmag * jnp.sin(d * da_im[None])
    lb_re = pw_re[:, :, :, None] * bb_re[None] - pw_im[:, :, :, None] * bb_im[None]
    lb_im = pw_re[:, :, :, None] * bb_im[None] + pw_im[:, :, :, None] * bb_re[None]
    hp = lax.Precision.HIGHEST
    kern = (jnp.einsum('gkp,dgpc->dgkc', c_re, lb_re[:tc], precision=hp)
            - jnp.einsum('gkp,dgpc->dgkc', c_im, lb_im[:tc], precision=hp))
    i_idx = jnp.arange(tc)
    lag = i_idx[None, :] - i_idx[:, None]
    m = jnp.where((lag >= 0)[:, :, None, None, None], kern[jnp.clip(lag, 0, tc - 1)], 0.0)
    skip = d_skip.reshape(SSM_GROUPS, SSM_GROUP_CH)
    eye_t = jnp.eye(tc, dtype=F32)
    eye_c = jnp.eye(SSM_GROUP_CH, dtype=F32)
    m = m + eye_t[:, :, None, None, None] * (skip[:, :, None] * eye_c[None])[None, None]
    m = jnp.transpose(m, (2, 0, 4, 1, 3)).reshape(SSM_GROUPS, SSM_CHUNK_W, SSM_CHUNK_W)
    wst = jnp.concatenate([lb_re[:tc][::-1], lb_im[:tc][::-1]], axis=2)
    wst = jnp.transpose(wst, (1, 0, 3, 2)).reshape(SSM_GROUPS, SSM_CHUNK_W, 2 * SSM_STATE)
    p1_re = pw_re[1:]
    p1_im = pw_im[1:]
    co_re = c_re[None] * p1_re[:, :, None, :] - c_im[None] * p1_im[:, :, None, :]
    co_im = -c_re[None] * p1_im[:, :, None, :] - c_im[None] * p1_re[:, :, None, :]
    cout = jnp.concatenate([co_re, co_im], axis=3)
    cout = jnp.transpose(cout, (1, 3, 0, 2)).reshape(SSM_GROUPS, 2 * SSM_STATE, SSM_CHUNK_W)
    lam_a = jnp.concatenate([pw_re[tc], pw_re[tc]], axis=1)
    lam_b = jnp.concatenate([-pw_im[tc], pw_im[tc]], axis=1)
    return m.astype(BF16), wst.astype(BF16), cout.astype(BF16), lam_a.astype(F32), lam_b.astype(F32)


def _mixer_call(x, pos_f, invf, gpre, win, ret_consts, wret, s5_ops, wglu, wout, gpost):
    bsz, seq, d_model = x.shape
    tm = min(MIXER_TILE, seq)
    assert seq % tm == 0 and tm % RET_CHUNK == 0
    rows = tm // SSM_CHUNK
    pitch = _state_row_pitch(rows)
    inner_decay, xi_full, zeta_full, cd_full = ret_consts
    m, wst, cout, lam_a, lam_b = s5_ops
    qkw = RET_HEADS * RET_QK_DIM
    vw = RET_HEADS * RET_V_DIM
    ssm_w = SSM_GROUPS * SSM_GROUP_CH
    consts = [invf, gpre, win, inner_decay, xi_full, zeta_full, cd_full, wret, m, wst, cout,
              lam_a, lam_b, wglu, wout, gpost]
    in_specs = [pl.BlockSpec((1, tm, d_model), lambda b, t: (b, t, 0)),
                pl.BlockSpec((1, tm, 1), lambda b, t: (b, t, 0))]
    in_specs += [_resident(c.shape) for c in consts]
    scratch = [
        pltpu.VMEM((RET_HEADS, RET_QK_DIM, RET_V_DIM), F32),
        pltpu.VMEM((SSM_GROUPS, 2 * SSM_STATE), F32),
        pltpu.VMEM((SSM_GROUPS, 2 * SSM_STATE), F32),
        pltpu.VMEM((tm, d_model), BF16),
        pltpu.VMEM((tm, qkw), BF16),
        pltpu.VMEM((tm, qkw), F32),
        pltpu.VMEM((tm, vw), BF16),
        pltpu.VMEM((tm, vw), BF16),
        pltpu.VMEM((tm, vw), BF16),
        pltpu.VMEM((ssm_w // LANES, tm, LANES), F32),
        pltpu.VMEM((SSM_GROUPS, rows, SSM_CHUNK_W), BF16),
        pltpu.VMEM((SSM_GROUPS * pitch, 2 * SSM_STATE), F32),
        pltpu.VMEM((SSM_GROUPS * pitch, 2 * SSM_STATE), F32),
        pltpu.VMEM((SSM_GROUPS * pitch, 2 * SSM_STATE), F32),
        pltpu.VMEM((SSM_GROUPS, rows, SSM_CHUNK_W), F32),
        pltpu.VMEM((ssm_w // LANES, tm, LANES), F32),
    ]
    return pl.pallas_call(
        _mixer_kernel,
        grid=(bsz, seq // tm),
        in_specs=in_specs,
        out_specs=pl.BlockSpec((1, tm, d_model), lambda b, t: (b, t, 0)),
        out_shape=jax.ShapeDtypeStruct(x.shape, F32),
        scratch_shapes=scratch,
        compiler_params=pltpu.CompilerParams(
            dimension_semantics=("arbitrary", "arbitrary"),
            vmem_limit_bytes=VMEM_LIMIT_BYTES),
        name="mixer",
    )(x, pos_f, *consts)


def _ffn_call(h, gpre, wg, wu, wd, gpost):
    n_tok, d_model = h.shape
    hidden = wg.shape[1]
    tm = min(FFN_TILE, n_tok)
    assert n_tok % tm == 0
    step = 1024
    hidden_chunks = tuple((lo, min(lo + step, hidden)) for lo in range(0, hidden, step))
    consts = [gpre, wg, wu, wd, gpost]
    return pl.pallas_call(
        functools.partial(_ffn_kernel, hidden_chunks=hidden_chunks),
        grid=(n_tok // tm,),
        in_specs=[pl.BlockSpec((tm, d_model), lambda i: (i, 0))] + [_resident(c.shape) for c in consts],
        out_specs=pl.BlockSpec((tm, d_model), lambda i: (i, 0)),
        out_shape=jax.ShapeDtypeStruct(h.shape, F32),
        compiler_params=pltpu.CompilerParams(
            dimension_semantics=("arbitrary",),
            vmem_limit_bytes=VMEM_LIMIT_BYTES),
        name="ffn",
    )(h, *consts)


def kernel(x, positions, mix_pre_norm, w_in, ssm_a_re, ssm_a_im, ssm_log_dt, ssm_b_re, ssm_b_im, ssm_c_re, ssm_c_im, ssm_d, w_glu_val, w_glu_gate, w_ret_up, w_out, mix_post_norm, ffn_pre_norm, w_ffn_gate, w_ffn_up, w_ffn_down, ffn_post_norm):
    bsz, seq, d_model = x.shape
    depth = w_in.shape[0]
    pos_f = positions.astype(F32).reshape(bsz, seq, 1)
    inv_freq = ROPE_BASE ** (-jnp.arange(0, RET_QK_DIM, 2, dtype=F32) / RET_QK_DIM)
    invf = jnp.concatenate([inv_freq, inv_freq]).reshape(1, RET_QK_DIM)
    ret_consts = _retention_constants()
    h = x
    for layer in range(depth):
        s5_ops = _s5_chunk_operators(ssm_a_re[layer], ssm_a_im[layer], ssm_log_dt[layer], ssm_b_re[layer],
                                     ssm_b_im[layer], ssm_c_re[layer], ssm_c_im[layer], ssm_d[layer])
        wglu = jnp.concatenate([w_glu_val[layer], w_glu_gate[layer]], axis=1).astype(BF16)
        h = _mixer_call(h, pos_f, invf, mix_pre_norm[layer].reshape(1, d_model), w_in[layer].astype(BF16),
                        ret_consts, w_ret_up[layer].astype(BF16), s5_ops, wglu, w_out[layer].astype(BF16),
                        mix_post_norm[layer].reshape(1, d_model))
        h = _ffn_call(h.reshape(bsz * seq, d_model), ffn_pre_norm[layer].reshape(1, d_model),
                      w_ffn_gate[layer].astype(BF16), w_ffn_up[layer].astype(BF16),
                      w_ffn_down[layer].astype(BF16), ffn_post_norm[layer].reshape(1, d_model)
                      ).reshape(bsz, seq, d_model)
    return h
```

```python
import functools

import jax
import jax.numpy as jnp
from jax import lax
from jax.experimental import pallas as pl
from jax.experimental.pallas import tpu as pltpu

F32 = jnp.float32
BF16 = jnp.bfloat16

EPS = 1e-6
ROPE_BASE = 10000.0
RET_HEADS = 4
RET_QK_DIM = 128
RET_V_DIM = 256
RET_CHUNK = 128
SSM_GROUP_CH = 16
SSM_GROUPS = 32
SSM_STATE = 64
SSM_CHUNK = 16
SSM_CHUNK_W = SSM_CHUNK * SSM_GROUP_CH
LANES = 128
GROUPS_PER_LANE_TILE = LANES // SSM_GROUP_CH
VMEM_LIMIT_BYTES = 58 * 1024 * 1024

MIXER_TILE = 512
FFN_TILE = 1024


def _dot(a, b):
    return jnp.dot(a, b, preferred_element_type=F32)


def _dot_nt(a, b):
    return lax.dot_general(a, b, (((1,), (1,)), ((), ())), preferred_element_type=F32)


def _sigmoid(x):
    return 0.5 * jnp.tanh(0.5 * x) + 0.5


def _rms_scale(x):
    return lax.rsqrt(jnp.mean(x * x, axis=-1, keepdims=True) + EPS)


def _transpose8(pieces, lane_block):
    a = list(pieces)
    s = 4
    while s >= 1:
        keep = (lane_block & s) == 0
        shift = SSM_GROUP_CH * s
        for i in range(8):
            if i & s:
                continue
            lo, hi = a[i], a[i + s]
            a[i] = jnp.where(keep, lo, pltpu.roll(hi, shift, 1))
            a[i + s] = jnp.where(keep, pltpu.roll(lo, LANES - shift, 1), hi)
        s //= 2
    return a


def _state_row_pitch(rows):
    assert rows % 8 == 0
    return rows + 4


def _mixer_kernel(x_ref, pos_ref, invf_ref, gpre_ref, win_ref, dec_ref, xi_ref, zeta_ref, cd_ref,
                  wret_ref, m_ref, wst_ref, cout_ref, lama_ref, lamb_ref, wglu_ref, wout_ref,
                  gpost_ref, o_ref,
                  rstate, sv, sw, u_scr, q_scr, k_scr, v_scr, sg_scr, gated_scr, ussm_scr, uc_scr,
                  s_scr, ssw_scr, xin_scr, yc_scr, ynat_scr):
    tm = x_ref.shape[1]
    n_chunks = tm // RET_CHUNK
    rows = tm // SSM_CHUNK
    pitch = _state_row_pitch(rows)
    qkw = RET_HEADS * RET_QK_DIM
    vw = RET_HEADS * RET_V_DIM
    ssm_w = SSM_GROUPS * SSM_GROUP_CH
    lane_tiles = ssm_w // LANES
    halves = SSM_CHUNK // 8
    d_model = x_ref.shape[2]
    hq = RET_QK_DIM // 2
    o_v = 2 * qkw
    o_gr = o_v + vw
    o_ssm = o_gr + vw
    o_ga = o_ssm + ssm_w
    o_gb = o_ga + d_model

    @pl.when(pl.program_id(1) == 0)
    def _():
        rstate[...] = jnp.zeros_like(rstate)
        sv[...] = jnp.zeros_like(sv)
        sw[...] = jnp.zeros_like(sw)

    def proj(lo, width):
        return _dot(u_scr[...], win_ref[:, lo:lo + width])

    lane_block = lax.broadcasted_iota(jnp.int32, (rows, LANES), 1) // SSM_GROUP_CH


    x = x_ref[0]
    u_scr[...] = (x * _rms_scale(x) * gpre_ref[...]).astype(BF16)

    u_ssm = proj(o_ssm, ssm_w)
    for qt in range(lane_tiles):
        ussm_scr[qt] = u_ssm[:, qt * LANES:(qt + 1) * LANES]

    q = proj(0, qkw)
    k = proj(qkw, qkw)

    for qt in range(lane_tiles):
        for half in range(halves):
            pieces = [ussm_scr[qt, pl.ds(8 * half + j, rows, stride=SSM_CHUNK), :] for j in range(8)]
            t = _transpose8(pieces, lane_block)
            for gl in range(GROUPS_PER_LANE_TILE):
                uc_scr[qt * GROUPS_PER_LANE_TILE + gl, :, half * LANES:(half + 1) * LANES] = t[gl].astype(BF16)

    half_tm = tm // 2
    first = lax.broadcasted_iota(jnp.int32, (half_tm, LANES), 1) < hq
    invf = invf_ref[...]
    ang = jnp.where(first, pos_ref[0, 0:half_tm, :] * invf, pos_ref[0, half_tm:tm, :] * invf)
    cs = jnp.cos(ang)
    sn = jnp.sin(ang)
    cs_sw = pltpu.roll(cs, hq, 1)
    sn_sw = pltpu.roll(sn, hq, 1)
    cos2 = jnp.concatenate([jnp.where(first, cs, cs_sw), jnp.where(first, cs_sw, cs)], axis=0)
    sin2 = jnp.concatenate([jnp.where(first, -sn, sn_sw), jnp.where(first, -sn_sw, sn)], axis=0)
    kscale = RET_QK_DIM ** -0.5
    for h in range(RET_HEADS):
        sl = slice(h * RET_QK_DIM, (h + 1) * RET_QK_DIM)
        qh = q[:, sl]
        kh = k[:, sl]
        q_scr[:, sl] = (qh * cos2 + pltpu.roll(qh, hq, 1) * sin2).astype(BF16)
        k_scr[:, sl] = (kh * cos2 + pltpu.roll(kh, hq, 1) * sin2) * kscale

    for g in range(SSM_GROUPS):
        s_g = _dot(uc_scr[g], wst_ref[g])
        s_scr[g * pitch:g * pitch + rows, :] = s_g
        ssw_scr[g * pitch:g * pitch + rows, :] = pltpu.roll(s_g, SSM_STATE, 1)

    v_scr[...] = proj(o_v, vw).astype(BF16)
    g_ret = proj(o_gr, vw)
    sg_scr[...] = (g_ret * _sigmoid(g_ret)).astype(BF16)

    gate_a = _sigmoid(proj(o_ga, d_model))
    gate_b = _sigmoid(proj(o_gb, d_model))

    lam_a = lama_ref[...]
    lam_b = lamb_ref[...]
    v = sv[...]
    w = sw[...]
    for n in range(rows):
        rsel = pl.ds(n, SSM_GROUPS, stride=pitch)
        xin_scr[rsel, :] = v
        v, w = (lam_a * v + lam_b * w + s_scr[rsel, :],
                lam_a * w - lam_b * v + ssw_scr[rsel, :])
    sv[...] = v
    sw[...] = w

    for g in range(SSM_GROUPS):
        x_in = xin_scr[g * pitch:g * pitch + rows, :].astype(BF16)
        yc_scr[g] = _dot(uc_scr[g], m_ref[g]) + _dot(x_in, cout_ref[g])

    for c in range(n_chunks):
        rs = slice(c * RET_CHUNK, (c + 1) * RET_CHUNK)
        qsl = [slice(h * RET_QK_DIM, (h + 1) * RET_QK_DIM) for h in range(RET_HEADS)]
        vsl = [slice(h * RET_V_DIM, (h + 1) * RET_V_DIM) for h in range(RET_HEADS)]
        qh = [q_scr[rs, qsl[h]] for h in range(RET_HEADS)]
        kf = [k_scr[rs, qsl[h]] for h in range(RET_HEADS)]
        vh = [v_scr[rs, vsl[h]] for h in range(RET_HEADS)]
        scores = [_dot_nt(qh[h], kf[h].astype(BF16)) * dec_ref[h] for h in range(RET_HEADS)]
        st = [rstate[h] for h in range(RET_HEADS)]
        cross = [_dot(qh[h], st[h].astype(BF16)) * xi_ref[h] for h in range(RET_HEADS)]
        for h in range(RET_HEADS):
            kz_t = (kf[h] * zeta_ref[h]).T.astype(BF16)
            rstate[h] = cd_ref[h] * st[h] + _dot(kz_t, vh[h])
        for h in range(RET_HEADS):
            ret = _dot(scores[h].astype(BF16), vh[h]) + cross[h]
            dev = ret - jnp.mean(ret, axis=-1, keepdims=True)
            rn = dev * lax.rsqrt(jnp.mean(dev * dev, axis=-1, keepdims=True) + EPS)
            gated_scr[rs, vsl[h]] = (rn * sg_scr[rs, vsl[h]].astype(F32)).astype(BF16)

    for qt in range(lane_tiles):
        for half in range(halves):
            pieces = [yc_scr[qt * GROUPS_PER_LANE_TILE + gl, :, half * LANES:(half + 1) * LANES]
                      for gl in range(GROUPS_PER_LANE_TILE)]
            t = _transpose8(pieces, lane_block)
            for i in range(8):
                ynat_scr[qt, pl.ds(8 * half + i, rows, stride=SSM_CHUNK), :] = t[i]
    z = jnp.concatenate([jax.nn.gelu(ynat_scr[qt]).astype(BF16) for qt in range(lane_tiles)], axis=1)

    glu = _dot(z, wglu_ref[...])
    y_b = glu[:, :d_model] * _sigmoid(glu[:, d_model:])
    y_a = _dot(gated_scr[...], wret_ref[...])

    mixed = _dot((gate_a * y_a + gate_b * y_b).astype(BF16), wout_ref[...])
    o_ref[0] = x_ref[0] + mixed * _rms_scale(mixed) * gpost_ref[...]


def _ffn_kernel(h_ref, gpre_ref, wg_ref, wu_ref, wd_ref, gpost_ref, o_ref, *, hidden_chunks):
    h = h_ref[...]
    f_in = (h * _rms_scale(h) * gpre_ref[...]).astype(BF16)
    f = None
    for lo, hi in hidden_chunks:
        gate = _dot(f_in, wg_ref[:, lo:hi])
        up = _dot(f_in, wu_ref[:, lo:hi])
        part = _dot((gate * _sigmoid(gate) * up).astype(BF16), wd_ref[lo:hi, :])
        f = part if f is None else f + part
    o_ref[...] = h_ref[...] + f * _rms_scale(f) * gpost_ref[...]


def _resident(shape):
    nd = len(shape)
    return pl.BlockSpec(shape, lambda *_: (0,) * nd, pipeline_mode=pl.Buffered(1))


def _retention_constants():
    c = RET_CHUNK
    log_g = jnp.log(1.0 - 2.0 ** (-5.0 - jnp.arange(RET_HEADS, dtype=F32)))
    idx = jnp.arange(c, dtype=F32)
    diff = idx[:, None] - idx[None, :]
    inner_decay = jnp.where(diff >= 0, jnp.exp(log_g[:, None, None] * jnp.maximum(diff, 0.0)), 0.0)
    zeta = jnp.exp(log_g[:, None] * (c - 1.0 - idx)[None, :])
    xi = jnp.exp(log_g[:, None] * (idx + 1.0)[None, :])
    chunk_decay = jnp.exp(log_g * c)
    zeta_full = jnp.broadcast_to(zeta[:, :, None], (RET_HEADS, c, RET_QK_DIM))
    xi_full = jnp.broadcast_to(xi[:, :, None], (RET_HEADS, c, RET_V_DIM))
    cd_full = jnp.broadcast_to(chunk_decay[:, None, None], (RET_HEADS, 1, RET_V_DIM))
    return inner_decay.astype(F32), xi_full.astype(F32), zeta_full.astype(F32), cd_full.astype(F32)


def _dot_f32(a, b):
    return jnp.dot(a, b, precision=lax.Precision.HIGHEST, preferred_element_type=F32)


def _s5_prep_kernel(arc_ref, aic_ref, arr_ref, air_ref, ldt_ref, btr_ref, bti_ref, ctr_ref, cti_ref,
                    dsk_ref, m_ref, wst_ref, cout_ref, lama_ref, lamb_ref):
    tc = SSM_CHUNK
    dt = jnp.exp(ldt_ref[...])

    def discretise(a_r, a_i):
        da_r = dt * a_r
        da_i = dt * a_i
        mag = jnp.exp(da_r)
        l_r = mag * jnp.cos(da_i)
        l_i = mag * jnp.sin(da_i)
        den = a_r * a_r + a_i * a_i
        num_r = l_r - 1.0
        return l_r, l_i, (num_r * a_r + l_i * a_i) / den, (l_i * a_r - num_r * a_i) / den

    lc_r, lc_i, _, _ = discretise(arc_ref[...], aic_ref[...])
    lr_r, lr_i, f_r, f_i = discretise(arr_ref[...], air_ref[...])
    bbt_r = f_r * btr_ref[...] - f_i * bti_ref[...]
    bbt_i = f_r * bti_ref[...] + f_i * btr_ref[...]

    lane_d = lax.broadcasted_iota(jnp.int32, (SSM_STATE, SSM_CHUNK_W), 1) // SSM_GROUP_CH
    pc_r = jnp.ones((SSM_STATE, 1), F32)
    pc_i = jnp.zeros((SSM_STATE, 1), F32)
    pw_r = jnp.zeros((SSM_STATE, SSM_CHUNK_W), F32)
    pw_i = jnp.zeros((SSM_STATE, SSM_CHUNK_W), F32)
    for d in range(tc):
        sel = lane_d == d
        pw_r = jnp.where(sel, pc_r, pw_r)
        pw_i = jnp.where(sel, pc_i, pw_i)
        pc_r, pc_i = pc_r * lc_r - pc_i * lc_i, pc_r * lc_i + pc_i * lc_r
    ct_r = ctr_ref[...]
    ct_i = cti_ref[...]

    r0 = (_dot_f32(bbt_r, pw_r * ct_r - pw_i * ct_i)
          - _dot_f32(bbt_i, pw_i * ct_r + pw_r * ct_i))
    row = lax.broadcasted_iota(jnp.int32, (SSM_GROUP_CH, SSM_CHUNK_W), 0)
    lane = lax.broadcasted_iota(jnp.int32, (SSM_GROUP_CH, SSM_CHUNK_W), 1)
    r0 = r0 + jnp.where(row == lane, dsk_ref[...], 0.0)
    for j in range(tc):
        blk = r0 if j == 0 else jnp.where(lane >= SSM_GROUP_CH * j, pltpu.roll(r0, SSM_GROUP_CH * j, 1), 0.0)
        m_ref[j * SSM_GROUP_CH:(j + 1) * SSM_GROUP_CH, :] = blk.astype(BF16)

    p1_r = pw_r * lc_r - pw_i * lc_i
    p1_i = pw_r * lc_i + pw_i * lc_r
    cout_ref[0:SSM_STATE, :] = (p1_r * ct_r - p1_i * ct_i).astype(BF16)
    cout_ref[SSM_STATE:2 * SSM_STATE, :] = (-(p1_i * ct_r + p1_r * ct_i)).astype(BF16)

    pr_r = jnp.ones((1, SSM_STATE), F32)
    pr_i = jnp.zeros((1, SSM_STATE), F32)
    powers = []
    for d in range(tc + 1):
        powers.append((pr_r, pr_i))
        pr_r, pr_i = pr_r * lr_r - pr_i * lr_i, pr_r * lr_i + pr_i * lr_r
    for j in range(tc):
        q_r, q_i = powers[tc - 1 - j]
        blk = jnp.concatenate([bbt_r * q_r - bbt_i * q_i, bbt_i * q_r + bbt_r * q_i], axis=1)
        wst_ref[j * SSM_GROUP_CH:(j + 1) * SSM_GROUP_CH, :] = blk.astype(BF16)
    q_r, q_i = powers[tc]
    lama_ref[...] = jnp.concatenate([q_r, q_r], axis=1)
    lamb_ref[...] = jnp.concatenate([-q_i, q_i], axis=1)


def _s5_prep_call(a_re, a_im, log_dt, b_re, b_im, c_re, c_im, d_skip):
    g, p, c = SSM_GROUPS, SSM_STATE, SSM_GROUP_CH
    tile_c = lambda m: jnp.tile(jnp.swapaxes(m, 1, 2), (1, 1, SSM_CHUNK))
    operands = [a_re.reshape(g, p, 1), a_im.reshape(g, p, 1), a_re.reshape(g, 1, p), a_im.reshape(g, 1, p),
                log_dt.reshape(g, 1, 1), jnp.swapaxes(b_re, 1, 2), jnp.swapaxes(b_im, 1, 2),
                tile_c(c_re), tile_c(c_im), jnp.tile(d_skip.reshape(g, 1, c), (1, 1, SSM_CHUNK))]
    out_shapes = [((SSM_CHUNK_W, SSM_CHUNK_W), BF16), ((SSM_CHUNK_W, 2 * p), BF16), ((2 * p, SSM_CHUNK_W), BF16),
                  ((1, 2 * p), F32), ((1, 2 * p), F32)]
    spec = lambda shp: pl.BlockSpec((None,) + tuple(shp), lambda i: (i, 0, 0))
    m, wst, cout, lam_a, lam_b = pl.pallas_call(
        _s5_prep_kernel,
        grid=(g,),
        in_specs=[spec(o.shape[1:]) for o in operands],
        out_specs=[spec(shp) for shp, _ in out_shapes],
        out_shape=[jax.ShapeDtypeStruct((g,) + shp, dt) for shp, dt in out_shapes],
        compiler_params=pltpu.CompilerParams(dimension_semantics=("arbitrary",)),
        name="s5_prep",
    )(*operands)
    return m, wst, cout, lam_a.reshape(g, 2 * p), lam_b.reshape(g, 2 * p)


def _mixer_call(x, pos_f, invf, gpre, win, ret_consts, wret, s5_ops, wglu, wout, gpost):
    bsz, seq, d_model = x.shape
    tm = min(MIXER_TILE, seq)
    assert seq % tm == 0 and tm % RET_CHUNK == 0
    rows = tm // SSM_CHUNK
    pitch = _state_row_pitch(rows)
    inner_decay, xi_full, zeta_full, cd_full = ret_consts
    m, wst, cout, lam_a, lam_b = s5_ops
    qkw = RET_HEADS * RET_QK_DIM
    vw = RET_HEADS * RET_V_DIM
    ssm_w = SSM_GROUPS * SSM_GROUP_CH
    consts = [invf, gpre, win, inner_decay, xi_full, zeta_full, cd_full, wret, m, wst, cout,
              lam_a, lam_b, wglu, wout, gpost]
    in_specs = [pl.BlockSpec((1, tm, d_model), lambda b, t: (b, t, 0)),
                pl.BlockSpec((1, tm, 1), lambda b, t: (b, t, 0))]
    in_specs += [_resident(c.shape) for c in consts]
    scratch = [
        pltpu.VMEM((RET_HEADS, RET_QK_DIM, RET_V_DIM), F32),
        pltpu.VMEM((SSM_GROUPS, 2 * SSM_STATE), F32),
        pltpu.VMEM((SSM_GROUPS, 2 * SSM_STATE), F32),
        pltpu.VMEM((tm, d_model), BF16),
        pltpu.VMEM((tm, qkw), BF16),
        pltpu.VMEM((tm, qkw), F32),
        pltpu.VMEM((tm, vw), BF16),
        pltpu.VMEM((tm, vw), BF16),
        pltpu.VMEM((tm, vw), BF16),
        pltpu.VMEM((ssm_w // LANES, tm, LANES), F32),
        pltpu.VMEM((SSM_GROUPS, rows, SSM_CHUNK_W), BF16),
        pltpu.VMEM((SSM_GROUPS * pitch, 2 * SSM_STATE), F32),
        pltpu.VMEM((SSM_GROUPS * pitch, 2 * SSM_STATE), F32),
        pltpu.VMEM((SSM_GROUPS * pitch, 2 * SSM_STATE), F32),
        pltpu.VMEM((SSM_GROUPS, rows, SSM_CHUNK_W), F32),
        pltpu.VMEM((ssm_w // LANES, tm, LANES), F32),
    ]
    return pl.pallas_call(
        _mixer_kernel,
        grid=(bsz, seq // tm),
        in_specs=in_specs,
        out_specs=pl.BlockSpec((1, tm, d_model), lambda b, t: (b, t, 0)),
        out_shape=jax.ShapeDtypeStruct(x.shape, F32),
        scratch_shapes=scratch,
        compiler_params=pltpu.CompilerParams(
            dimension_semantics=("arbitrary", "arbitrary"),
            vmem_limit_bytes=VMEM_LIMIT_BYTES),
        name="mixer",
    )(x, pos_f, *consts)


def _ffn_call(h, gpre, wg, wu, wd, gpost):
    n_tok, d_model = h.shape
    hidden = wg.shape[1]
    tm = min(FFN_TILE, n_tok)
    assert n_tok % tm == 0
    step = 1024
    hidden_chunks = tuple((lo, min(lo + step, hidden)) for lo in range(0, hidden, step))
    consts = [gpre, wg, wu, wd, gpost]
    return pl.pallas_call(
        functools.partial(_ffn_kernel, hidden_chunks=hidden_chunks),
        grid=(n_tok // tm,),
        in_specs=[pl.BlockSpec((tm, d_model), lambda i: (i, 0))] + [_resident(c.shape) for c in consts],
        out_specs=pl.BlockSpec((tm, d_model), lambda i: (i, 0)),
        out_shape=jax.ShapeDtypeStruct(h.shape, F32),
        compiler_params=pltpu.CompilerParams(
            dimension_semantics=("arbitrary",),
            vmem_limit_bytes=VMEM_LIMIT_BYTES),
        name="ffn",
    )(h, *consts)


def kernel(x, positions, mix_pre_norm, w_in, ssm_a_re, ssm_a_im, ssm_log_dt, ssm_b_re, ssm_b_im, ssm_c_re, ssm_c_im, ssm_d, w_glu_val, w_glu_gate, w_ret_up, w_out, mix_post_norm, ffn_pre_norm, w_ffn_gate, w_ffn_up, w_ffn_down, ffn_post_norm):
    bsz, seq, d_model = x.shape
    depth = w_in.shape[0]
    pos_f = positions.astype(F32).reshape(bsz, seq, 1)
    inv_freq = ROPE_BASE ** (-jnp.arange(0, RET_QK_DIM, 2, dtype=F32) / RET_QK_DIM)
    invf = jnp.concatenate([inv_freq, inv_freq]).reshape(1, RET_QK_DIM)
    ret_consts = _retention_constants()
    h = x
    for layer in range(depth):
        s5_ops = _s5_prep_call(ssm_a_re[layer], ssm_a_im[layer], ssm_log_dt[layer], ssm_b_re[layer],
                               ssm_b_im[layer], ssm_c_re[layer], ssm_c_im[layer], ssm_d[layer])
        wglu = jnp.concatenate([w_glu_val[layer], w_glu_gate[layer]], axis=1).astype(BF16)
        h = _mixer_call(h, pos_f, invf, mix_pre_norm[layer].reshape(1, d_model), w_in[layer].astype(BF16),
                        ret_consts, w_ret_up[layer].astype(BF16), s5_ops, wglu, w_out[layer].astype(BF16),
                        mix_post_norm[layer].reshape(1, d_model))
        h = _ffn_call(h.reshape(bsz * seq, d_model), ffn_pre_norm[layer].reshape(1, d_model),
                      w_ffn_gate[layer].astype(BF16), w_ffn_up[layer].astype(BF16),
                      w_ffn_down[layer].astype(BF16), ffn_post_norm[layer].reshape(1, d_model)
                      ).reshape(bsz, seq, d_model)
    return h
```

```python
import functools

import jax
import jax.numpy as jnp
from jax import lax
from jax.experimental import pallas as pl
from jax.experimental.pallas import tpu as pltpu

F32 = jnp.float32
BF16 = jnp.bfloat16

EPS = 1e-6
ROPE_BASE = 10000.0
RET_HEADS = 4
RET_QK_DIM = 128
RET_V_DIM = 256
RET_CHUNK = 128
SSM_GROUP_CH = 16
SSM_GROUPS = 32
SSM_STATE = 64
SSM_CHUNK = 8
SSM_CHUNK_W = SSM_CHUNK * SSM_GROUP_CH
LANES = 128
GROUPS_PER_LANE_TILE = LANES // SSM_GROUP_CH
VMEM_LIMIT_BYTES = 58 * 1024 * 1024

MIXER_TILE = 512
FFN_TILE = 1024


def _dot(a, b):
    return jnp.dot(a, b, preferred_element_type=F32)


def _dot_nt(a, b):
    return lax.dot_general(a, b, (((1,), (1,)), ((), ())), preferred_element_type=F32)


def _sigmoid(x):
    return 0.5 * jnp.tanh(0.5 * x) + 0.5


def _rms_scale(x):
    return lax.rsqrt(jnp.mean(x * x, axis=-1, keepdims=True) + EPS)


def _transpose8(pieces, lane_block):
    a = list(pieces)
    s = 4
    while s >= 1:
        keep = (lane_block & s) == 0
        shift = SSM_GROUP_CH * s
        for i in range(8):
            if i & s:
                continue
            lo, hi = a[i], a[i + s]
            a[i] = jnp.where(keep, lo, pltpu.roll(hi, shift, 1))
            a[i + s] = jnp.where(keep, pltpu.roll(lo, LANES - shift, 1), hi)
        s //= 2
    return a


def _state_row_pitch(rows):
    assert rows % 8 == 0
    return rows + 4


def _mixer_kernel(x_ref, pos_ref, invf_ref, gpre_ref, win_ref, dec_ref, xi_ref, zeta_ref, cd_ref,
                  wret_ref, m_ref, wst_ref, cout_ref, lama_ref, lamb_ref, wglu_ref, wout_ref,
                  gpost_ref, o_ref,
                  rstate, sv, sw, u_scr, q_scr, k_scr, v_scr, sg_scr, gated_scr, ussm_scr, uc_scr,
                  s_scr, ssw_scr, xin_scr, yc_scr, ynat_scr):
    tm = x_ref.shape[1]
    n_chunks = tm // RET_CHUNK
    rows = tm // SSM_CHUNK
    pitch = _state_row_pitch(rows)
    qkw = RET_HEADS * RET_QK_DIM
    vw = RET_HEADS * RET_V_DIM
    ssm_w = SSM_GROUPS * SSM_GROUP_CH
    lane_tiles = ssm_w // LANES
    halves = SSM_CHUNK // 8
    d_model = x_ref.shape[2]
    hq = RET_QK_DIM // 2
    o_v = 2 * qkw
    o_gr = o_v + vw
    o_ssm = o_gr + vw
    o_ga = o_ssm + ssm_w
    o_gb = o_ga + d_model

    @pl.when(pl.program_id(1) == 0)
    def _():
        rstate[...] = jnp.zeros_like(rstate)
        sv[...] = jnp.zeros_like(sv)
        sw[...] = jnp.zeros_like(sw)

    def proj(lo, width):
        return _dot(u_scr[...], win_ref[:, lo:lo + width])

    lane_block = lax.broadcasted_iota(jnp.int32, (rows, LANES), 1) // SSM_GROUP_CH


    x = x_ref[0]
    u_scr[...] = (x * _rms_scale(x) * gpre_ref[...]).astype(BF16)

    u_ssm = proj(o_ssm, ssm_w)
    for qt in range(lane_tiles):
        ussm_scr[qt] = u_ssm[:, qt * LANES:(qt + 1) * LANES]

    q = proj(0, qkw)
    k = proj(qkw, qkw)

    for qt in range(lane_tiles):
        for half in range(halves):
            pieces = [ussm_scr[qt, pl.ds(8 * half + j, rows, stride=SSM_CHUNK), :] for j in range(8)]
            t = _transpose8(pieces, lane_block)
            for gl in range(GROUPS_PER_LANE_TILE):
                uc_scr[qt * GROUPS_PER_LANE_TILE + gl, :, half * LANES:(half + 1) * LANES] = t[gl].astype(BF16)

    half_tm = tm // 2
    pos_rows = pos_ref[0, 0]
    half_rows = half_tm // LANES
    pos_lo = jnp.concatenate([pos_rows[r:r + 1, :] for r in range(half_rows)], axis=1)
    pos_hi = jnp.concatenate([pos_rows[r:r + 1, :] for r in range(half_rows, 2 * half_rows)], axis=1)
    upper = lax.broadcasted_iota(jnp.int32, (LANES, half_tm), 0) < hq
    ang_t = invf_ref[...] * jnp.where(upper, pos_lo, pos_hi)
    cs = jnp.cos(ang_t).T
    sn = jnp.sin(ang_t).T
    first = lax.broadcasted_iota(jnp.int32, (half_tm, LANES), 1) < hq
    cs_sw = pltpu.roll(cs, hq, 1)
    sn_sw = pltpu.roll(sn, hq, 1)
    cos2 = jnp.concatenate([jnp.where(first, cs, cs_sw), jnp.where(first, cs_sw, cs)], axis=0)
    sin2 = jnp.concatenate([jnp.where(first, -sn, sn_sw), jnp.where(first, -sn_sw, sn)], axis=0)
    kscale = RET_QK_DIM ** -0.5
    for h in range(RET_HEADS):
        sl = slice(h * RET_QK_DIM, (h + 1) * RET_QK_DIM)
        qh = q[:, sl]
        kh = k[:, sl]
        q_scr[:, sl] = (qh * cos2 + pltpu.roll(qh, hq, 1) * sin2).astype(BF16)
        k_scr[:, sl] = (kh * cos2 + pltpu.roll(kh, hq, 1) * sin2) * kscale

    for g in range(SSM_GROUPS):
        s_g = _dot(uc_scr[g], wst_ref[g])
        s_scr[g * pitch:g * pitch + rows, :] = s_g
        ssw_scr[g * pitch:g * pitch + rows, :] = pltpu.roll(s_g, SSM_STATE, 1)

    v_scr[...] = proj(o_v, vw).astype(BF16)
    g_ret = proj(o_gr, vw)
    sg_scr[...] = (g_ret * _sigmoid(g_ret)).astype(BF16)

    gate_a = _sigmoid(proj(o_ga, d_model))
    gate_b = _sigmoid(proj(o_gb, d_model))

    lam_a = lama_ref[...]
    lam_b = lamb_ref[...]
    v = sv[...]
    w = sw[...]
    for n in range(rows):
        rsel = pl.ds(n, SSM_GROUPS, stride=pitch)
        xin_scr[rsel, :] = v
        v, w = (lam_a * v + lam_b * w + s_scr[rsel, :],
                lam_a * w - lam_b * v + ssw_scr[rsel, :])
    sv[...] = v
    sw[...] = w

    for g in range(SSM_GROUPS):
        x_in = xin_scr[g * pitch:g * pitch + rows, :].astype(BF16)
        yc_scr[g] = _dot(uc_scr[g], m_ref[g]) + _dot(x_in, cout_ref[g])

    for c in range(n_chunks):
        rs = slice(c * RET_CHUNK, (c + 1) * RET_CHUNK)
        qsl = [slice(h * RET_QK_DIM, (h + 1) * RET_QK_DIM) for h in range(RET_HEADS)]
        vsl = [slice(h * RET_V_DIM, (h + 1) * RET_V_DIM) for h in range(RET_HEADS)]
        qh = [q_scr[rs, qsl[h]] for h in range(RET_HEADS)]
        kf = [k_scr[rs, qsl[h]] for h in range(RET_HEADS)]
        vh = [v_scr[rs, vsl[h]] for h in range(RET_HEADS)]
        scores = [_dot_nt(qh[h], kf[h].astype(BF16)) * dec_ref[h] for h in range(RET_HEADS)]
        st = [rstate[h] for h in range(RET_HEADS)]
        cross = [_dot(qh[h], st[h].astype(BF16)) * xi_ref[h] for h in range(RET_HEADS)]
        for h in range(RET_HEADS):
            kz_t = (kf[h] * zeta_ref[h]).T.astype(BF16)
            rstate[h] = cd_ref[h] * st[h] + _dot(kz_t, vh[h])
        for h in range(RET_HEADS):
            ret = _dot(scores[h].astype(BF16), vh[h]) + cross[h]
            dev = ret - jnp.mean(ret, axis=-1, keepdims=True)
            rn = dev * lax.rsqrt(jnp.mean(dev * dev, axis=-1, keepdims=True) + EPS)
            gated_scr[rs, vsl[h]] = (rn * sg_scr[rs, vsl[h]].astype(F32)).astype(BF16)

    for qt in range(lane_tiles):
        for half in range(halves):
            pieces = [yc_scr[qt * GROUPS_PER_LANE_TILE + gl, :, half * LANES:(half + 1) * LANES]
                      for gl in range(GROUPS_PER_LANE_TILE)]
            t = _transpose8(pieces, lane_block)
            for i in range(8):
                ynat_scr[qt, pl.ds(8 * half + i, rows, stride=SSM_CHUNK), :] = t[i]
    z = jnp.concatenate([jax.nn.gelu(ynat_scr[qt]).astype(BF16) for qt in range(lane_tiles)], axis=1)

    glu = _dot(z, wglu_ref[...])
    y_b = glu[:, :d_model] * _sigmoid(glu[:, d_model:])
    y_a = _dot(gated_scr[...], wret_ref[...])

    mixed = _dot((gate_a * y_a + gate_b * y_b).astype(BF16), wout_ref[...])
    o_ref[0] = x_ref[0] + mixed * _rms_scale(mixed) * gpost_ref[...]


def _ffn_kernel(h_ref, gpre_ref, wg_ref, wu_ref, wd_ref, gpost_ref, o_ref, *, hidden_chunks):
    h = h_ref[...]
    f_in = (h * _rms_scale(h) * gpre_ref[...]).astype(BF16)
    f = None
    for lo, hi in hidden_chunks:
        gate = _dot(f_in, wg_ref[:, lo:hi])
        up = _dot(f_in, wu_ref[:, lo:hi])
        part = _dot((gate * _sigmoid(gate) * up).astype(BF16), wd_ref[lo:hi, :])
        f = part if f is None else f + part
    o_ref[...] = h_ref[...] + f * _rms_scale(f) * gpost_ref[...]


def _resident(shape):
    nd = len(shape)
    return pl.BlockSpec(shape, lambda *_: (0,) * nd, pipeline_mode=pl.Buffered(1))


def _retention_constants():
    c = RET_CHUNK
    log_g = jnp.log(1.0 - 2.0 ** (-5.0 - jnp.arange(RET_HEADS, dtype=F32)))
    idx = jnp.arange(c, dtype=F32)
    diff = idx[:, None] - idx[None, :]
    inner_decay = jnp.where(diff >= 0, jnp.exp(log_g[:, None, None] * jnp.maximum(diff, 0.0)), 0.0)
    zeta = jnp.exp(log_g[:, None] * (c - 1.0 - idx)[None, :])
    xi = jnp.exp(log_g[:, None] * (idx + 1.0)[None, :])
    chunk_decay = jnp.exp(log_g * c)
    zeta_full = jnp.broadcast_to(zeta[:, :, None], (RET_HEADS, c, RET_QK_DIM))
    xi_full = jnp.broadcast_to(xi[:, :, None], (RET_HEADS, c, RET_V_DIM))
    cd_full = jnp.broadcast_to(chunk_decay[:, None, None], (RET_HEADS, 1, RET_V_DIM))
    return inner_decay.astype(F32), xi_full.astype(F32), zeta_full.astype(F32), cd_full.astype(F32)


def _dot_f32(a, b):
    return jnp.dot(a, b, precision=lax.Precision.HIGHEST, preferred_element_type=F32)


def _s5_prep_kernel(arc_ref, aic_ref, arr_ref, air_ref, ldt_ref, btr_ref, bti_ref, ctr_ref, cti_ref,
                    dsk_ref, m_ref, wst_ref, cout_ref, lama_ref, lamb_ref):
    tc = SSM_CHUNK
    dt = jnp.exp(ldt_ref[...])

    def discretise(a_r, a_i):
        da_r = dt * a_r
        da_i = dt * a_i
        mag = jnp.exp(da_r)
        l_r = mag * jnp.cos(da_i)
        l_i = mag * jnp.sin(da_i)
        den = a_r * a_r + a_i * a_i
        num_r = l_r - 1.0
        return l_r, l_i, (num_r * a_r + l_i * a_i) / den, (l_i * a_r - num_r * a_i) / den

    lc_r, lc_i, _, _ = discretise(arc_ref[...], aic_ref[...])
    lr_r, lr_i, f_r, f_i = discretise(arr_ref[...], air_ref[...])
    bbt_r = f_r * btr_ref[...] - f_i * bti_ref[...]
    bbt_i = f_r * bti_ref[...] + f_i * btr_ref[...]

    lane_d = lax.broadcasted_iota(jnp.int32, (SSM_STATE, SSM_CHUNK_W), 1) // SSM_GROUP_CH
    pc_r = jnp.ones((SSM_STATE, 1), F32)
    pc_i = jnp.zeros((SSM_STATE, 1), F32)
    pw_r = jnp.zeros((SSM_STATE, SSM_CHUNK_W), F32)
    pw_i = jnp.zeros((SSM_STATE, SSM_CHUNK_W), F32)
    for d in range(tc):
        sel = lane_d == d
        pw_r = jnp.where(sel, pc_r, pw_r)
        pw_i = jnp.where(sel, pc_i, pw_i)
        pc_r, pc_i = pc_r * lc_r - pc_i * lc_i, pc_r * lc_i + pc_i * lc_r
    ct_r = ctr_ref[...]
    ct_i = cti_ref[...]

    r0 = (_dot_f32(bbt_r, pw_r * ct_r - pw_i * ct_i)
          - _dot_f32(bbt_i, pw_i * ct_r + pw_r * ct_i))
    row = lax.broadcasted_iota(jnp.int32, (SSM_GROUP_CH, SSM_CHUNK_W), 0)
    lane = lax.broadcasted_iota(jnp.int32, (SSM_GROUP_CH, SSM_CHUNK_W), 1)
    r0 = r0 + jnp.where(row == lane, dsk_ref[...], 0.0)
    for j in range(tc):
        blk = r0 if j == 0 else jnp.where(lane >= SSM_GROUP_CH * j, pltpu.roll(r0, SSM_GROUP_CH * j, 1), 0.0)
        m_ref[j * SSM_GROUP_CH:(j + 1) * SSM_GROUP_CH, :] = blk.astype(BF16)

    p1_r = pw_r * lc_r - pw_i * lc_i
    p1_i = pw_r * lc_i + pw_i * lc_r
    cout_ref[0:SSM_STATE, :] = (p1_r * ct_r - p1_i * ct_i).astype(BF16)
    cout_ref[SSM_STATE:2 * SSM_STATE, :] = (-(p1_i * ct_r + p1_r * ct_i)).astype(BF16)

    pr_r = jnp.ones((1, SSM_STATE), F32)
    pr_i = jnp.zeros((1, SSM_STATE), F32)
    powers = []
    for d in range(tc + 1):
        powers.append((pr_r, pr_i))
        pr_r, pr_i = pr_r * lr_r - pr_i * lr_i, pr_r * lr_i + pr_i * lr_r
    for j in range(tc):
        q_r, q_i = powers[tc - 1 - j]
        blk = jnp.concatenate([bbt_r * q_r - bbt_i * q_i, bbt_i * q_r + bbt_r * q_i], axis=1)
        wst_ref[j * SSM_GROUP_CH:(j + 1) * SSM_GROUP_CH, :] = blk.astype(BF16)
    q_r, q_i = powers[tc]
    lama_ref[...] = jnp.concatenate([q_r, q_r], axis=1)
    lamb_ref[...] = jnp.concatenate([-q_i, q_i], axis=1)


def _s5_prep_call(a_re, a_im, log_dt, b_re, b_im, c_re, c_im, d_skip):
    g, p, c = SSM_GROUPS, SSM_STATE, SSM_GROUP_CH
    tile_c = lambda m: jnp.tile(jnp.swapaxes(m, 1, 2), (1, 1, SSM_CHUNK))
    operands = [a_re.reshape(g, p, 1), a_im.reshape(g, p, 1), a_re.reshape(g, 1, p), a_im.reshape(g, 1, p),
                log_dt.reshape(g, 1, 1), jnp.swapaxes(b_re, 1, 2), jnp.swapaxes(b_im, 1, 2),
                tile_c(c_re), tile_c(c_im), jnp.tile(d_skip.reshape(g, 1, c), (1, 1, SSM_CHUNK))]
    out_shapes = [((SSM_CHUNK_W, SSM_CHUNK_W), BF16), ((SSM_CHUNK_W, 2 * p), BF16), ((2 * p, SSM_CHUNK_W), BF16),
                  ((1, 2 * p), F32), ((1, 2 * p), F32)]
    spec = lambda shp: pl.BlockSpec((None,) + tuple(shp), lambda i: (i, 0, 0))
    m, wst, cout, lam_a, lam_b = pl.pallas_call(
        _s5_prep_kernel,
        grid=(g,),
        in_specs=[spec(o.shape[1:]) for o in operands],
        out_specs=[spec(shp) for shp, _ in out_shapes],
        out_shape=[jax.ShapeDtypeStruct((g,) + shp, dt) for shp, dt in out_shapes],
        compiler_params=pltpu.CompilerParams(dimension_semantics=("arbitrary",)),
        name="s5_prep",
    )(*operands)
    return m, wst, cout, lam_a.reshape(g, 2 * p), lam_b.reshape(g, 2 * p)


def _mixer_call(x, pos_f, invf, gpre, win, ret_consts, wret, s5_ops, wglu, wout, gpost):
    bsz, seq, d_model = x.shape
    tm = min(MIXER_TILE, seq)
    assert seq % tm == 0 and tm % RET_CHUNK == 0
    rows = tm // SSM_CHUNK
    pitch = _state_row_pitch(rows)
    inner_decay, xi_full, zeta_full, cd_full = ret_consts
    m, wst, cout, lam_a, lam_b = s5_ops
    qkw = RET_HEADS * RET_QK_DIM
    vw = RET_HEADS * RET_V_DIM
    ssm_w = SSM_GROUPS * SSM_GROUP_CH
    consts = [invf, gpre, win, inner_decay, xi_full, zeta_full, cd_full, wret, m, wst, cout,
              lam_a, lam_b, wglu, wout, gpost]
    in_specs = [pl.BlockSpec((1, tm, d_model), lambda b, t: (b, t, 0)),
                pl.BlockSpec((1, 1, tm // LANES, LANES), lambda b, t: (b, t, 0, 0))]
    in_specs += [_resident(c.shape) for c in consts]
    scratch = [
        pltpu.VMEM((RET_HEADS, RET_QK_DIM, RET_V_DIM), F32),
        pltpu.VMEM((SSM_GROUPS, 2 * SSM_STATE), F32),
        pltpu.VMEM((SSM_GROUPS, 2 * SSM_STATE), F32),
        pltpu.VMEM((tm, d_model), BF16),
        pltpu.VMEM((tm, qkw), BF16),
        pltpu.VMEM((tm, qkw), F32),
        pltpu.VMEM((tm, vw), BF16),
        pltpu.VMEM((tm, vw), BF16),
        pltpu.VMEM((tm, vw), BF16),
        pltpu.VMEM((ssm_w // LANES, tm, LANES), F32),
        pltpu.VMEM((SSM_GROUPS, rows, SSM_CHUNK_W), BF16),
        pltpu.VMEM((SSM_GROUPS * pitch, 2 * SSM_STATE), F32),
        pltpu.VMEM((SSM_GROUPS * pitch, 2 * SSM_STATE), F32),
        pltpu.VMEM((SSM_GROUPS * pitch, 2 * SSM_STATE), F32),
        pltpu.VMEM((SSM_GROUPS, rows, SSM_CHUNK_W), F32),
        pltpu.VMEM((ssm_w // LANES, tm, LANES), F32),
    ]
    return pl.pallas_call(
        _mixer_kernel,
        grid=(bsz, seq // tm),
        in_specs=in_specs,
        out_specs=pl.BlockSpec((1, tm, d_model), lambda b, t: (b, t, 0)),
        out_shape=jax.ShapeDtypeStruct(x.shape, F32),
        scratch_shapes=scratch,
        compiler_params=pltpu.CompilerParams(
            dimension_semantics=("arbitrary", "arbitrary"),
            vmem_limit_bytes=VMEM_LIMIT_BYTES),
        name="mixer",
    )(x, pos_f.reshape(bsz, seq // tm, tm // LANES, LANES), *consts)


def _ffn_call(h, gpre, wg, wu, wd, gpost):
    n_tok, d_model = h.shape
    hidden = wg.shape[1]
    tm = min(FFN_TILE, n_tok)
    assert n_tok % tm == 0
    step = 1024
    hidden_chunks = tuple((lo, min(lo + step, hidden)) for lo in range(0, hidden, step))
    consts = [gpre, wg, wu, wd, gpost]
    return pl.pallas_call(
        functools.partial(_ffn_kernel, hidden_chunks=hidden_chunks),
        grid=(n_tok // tm,),
        in_specs=[pl.BlockSpec((tm, d_model), lambda i: (i, 0))] + [_resident(c.shape) for c in consts],
        out_specs=pl.BlockSpec((tm, d_model), lambda i: (i, 0)),
        out_shape=jax.ShapeDtypeStruct(h.shape, F32),
        compiler_params=pltpu.CompilerParams(
            dimension_semantics=("arbitrary",),
            vmem_limit_bytes=VMEM_LIMIT_BYTES),
        name="ffn",
    )(h, *consts)


def kernel(x, positions, mix_pre_norm, w_in, ssm_a_re, ssm_a_im, ssm_log_dt, ssm_b_re, ssm_b_im, ssm_c_re, ssm_c_im, ssm_d, w_glu_val, w_glu_gate, w_ret_up, w_out, mix_post_norm, ffn_pre_norm, w_ffn_gate, w_ffn_up, w_ffn_down, ffn_post_norm):
    bsz, seq, d_model = x.shape
    depth = w_in.shape[0]
    pos_f = positions.astype(F32)
    inv_freq = ROPE_BASE ** (-jnp.arange(0, RET_QK_DIM, 2, dtype=F32) / RET_QK_DIM)
    invf = jnp.concatenate([inv_freq, inv_freq]).reshape(RET_QK_DIM, 1)
    ret_consts = _retention_constants()
    h = x
    for layer in range(depth):
        s5_ops = _s5_prep_call(ssm_a_re[layer], ssm_a_im[layer], ssm_log_dt[layer], ssm_b_re[layer],
                               ssm_b_im[layer], ssm_c_re[layer], ssm_c_im[layer], ssm_d[layer])
        wglu = jnp.concatenate([w_glu_val[layer], w_glu_gate[layer]], axis=1).astype(BF16)
        h = _mixer_call(h, pos_f, invf, mix_pre_norm[layer].reshape(1, d_model), w_in[layer].astype(BF16),
                        ret_consts, w_ret_up[layer].astype(BF16), s5_ops, wglu, w_out[layer].astype(BF16),
                        mix_post_norm[layer].reshape(1, d_model))
        h = _ffn_call(h.reshape(bsz * seq, d_model), ffn_pre_norm[layer].reshape(1, d_model),
                      w_ffn_gate[layer].astype(BF16), w_ffn_up[layer].astype(BF16),
                      w_ffn_down[layer].astype(BF16), ffn_post_norm[layer].reshape(1, d_model)
                      ).reshape(bsz, seq, d_model)
    return h
```

```python
import functools

import jax
import jax.numpy as jnp
from jax import lax
from jax.experimental import pallas as pl
from jax.experimental.pallas import tpu as pltpu

F32 = jnp.float32
BF16 = jnp.bfloat16

EPS = 1e-6
ROPE_BASE = 10000.0
RET_HEADS = 4
RET_QK_DIM = 128
RET_V_DIM = 256
RET_CHUNK = 128
SSM_GROUP_CH = 16
SSM_GROUPS = 32
SSM_STATE = 64
SSM_CHUNK = 8
SSM_CHUNK_W = SSM_CHUNK * SSM_GROUP_CH
LANES = 128
GROUPS_PER_LANE_TILE = LANES // SSM_GROUP_CH
VMEM_LIMIT_BYTES = 58 * 1024 * 1024

MIXER_TILE = 1024
MIXER_SUB_TILE = 512
FFN_TILE = 1024
FFN_ROW_BLOCK = 256


def _dot(a, b):
    return jnp.dot(a, b, preferred_element_type=F32)


def _dot_nt(a, b):
    return lax.dot_general(a, b, (((1,), (1,)), ((), ())), preferred_element_type=F32)


def _sigmoid(x):
    return 0.5 * jnp.tanh(0.5 * x) + 0.5


def _rms_scale(x):
    return lax.rsqrt(jnp.mean(x * x, axis=-1, keepdims=True) + EPS)


def _transpose8(pieces, lane_block):
    a = list(pieces)
    s = 4
    while s >= 1:
        keep = (lane_block & s) == 0
        shift = SSM_GROUP_CH * s
        for i in range(8):
            if i & s:
                continue
            lo, hi = a[i], a[i + s]
            a[i] = jnp.where(keep, lo, pltpu.roll(hi, shift, 1))
            a[i + s] = jnp.where(keep, pltpu.roll(lo, LANES - shift, 1), hi)
        s //= 2
    return a


def _state_row_pitch(rows):
    assert rows % 8 == 0
    return rows + 4


def _mixer_kernel(*refs, tm):
    x_ref = refs[0]
    rstate, sv, sw = refs[19:22]

    @pl.when(pl.program_id(1) == 0)
    def _():
        rstate[...] = jnp.zeros_like(rstate)
        sv[...] = jnp.zeros_like(sv)
        sw[...] = jnp.zeros_like(sw)

    for base in range(0, x_ref.shape[1], tm):
        _mixer_sub_tile(base, tm, *refs)


def _mixer_sub_tile(base, tm, x_ref, pos_ref, invf_ref, gpre_ref, win_ref, dec_ref, xi_ref, zeta_ref, cd_ref,
                    wret_ref, m_ref, wst_ref, cout_ref, lama_ref, lamb_ref, wglu_ref, wout_ref,
                    gpost_ref, o_ref,
                    rstate, sv, sw, u_scr, q_scr, k_scr, v_scr, sg_scr, gated_scr, ussm_scr, uc_scr,
                    s_scr, ssw_scr, xin_scr, yc_scr, ynat_scr):
    n_chunks = tm // RET_CHUNK
    rows = tm // SSM_CHUNK
    pitch = _state_row_pitch(rows)
    qkw = RET_HEADS * RET_QK_DIM
    vw = RET_HEADS * RET_V_DIM
    ssm_w = SSM_GROUPS * SSM_GROUP_CH
    lane_tiles = ssm_w // LANES
    halves = SSM_CHUNK // 8
    d_model = x_ref.shape[2]
    hq = RET_QK_DIM // 2
    o_v = 2 * qkw
    o_gr = o_v + vw
    o_ssm = o_gr + vw
    o_ga = o_ssm + ssm_w
    o_gb = o_ga + d_model

    def tile_rows(rs):
        return slice(base + rs.start, base + rs.stop)

    def proj(lo, width):
        return _dot(u_scr[...], win_ref[:, lo:lo + width])

    lane_block = lax.broadcasted_iota(jnp.int32, (rows, LANES), 1) // SSM_GROUP_CH


    row_halves = [slice(0, tm // 2), slice(tm // 2, tm)]
    for rs in row_halves:
        x = x_ref[0, tile_rows(rs), :]
        u_scr[rs, :] = (x * _rms_scale(x) * gpre_ref[...]).astype(BF16)

    for rs in row_halves:
        u_ssm = _dot(u_scr[rs, :], win_ref[:, o_ssm:o_ssm + ssm_w])
        for qt in range(lane_tiles):
            ussm_scr[qt, rs, :] = u_ssm[:, qt * LANES:(qt + 1) * LANES]

    q = proj(0, qkw)
    k = proj(qkw, qkw)

    for qt in range(lane_tiles):
        for half in range(halves):
            pieces = [ussm_scr[qt, pl.ds(8 * half + j, rows, stride=SSM_CHUNK), :] for j in range(8)]
            t = _transpose8(pieces, lane_block)
            for gl in range(GROUPS_PER_LANE_TILE):
                uc_scr[qt * GROUPS_PER_LANE_TILE + gl, :, half * LANES:(half + 1) * LANES] = t[gl].astype(BF16)

    half_tm = tm // 2
    pos_rows = pos_ref[0, 0, base // LANES:(base + tm) // LANES, :]
    half_rows = half_tm // LANES
    pos_lo = jnp.concatenate([pos_rows[r:r + 1, :] for r in range(half_rows)], axis=1)
    pos_hi = jnp.concatenate([pos_rows[r:r + 1, :] for r in range(half_rows, 2 * half_rows)], axis=1)
    upper = lax.broadcasted_iota(jnp.int32, (LANES, half_tm), 0) < hq
    ang_t = invf_ref[...] * jnp.where(upper, pos_lo, pos_hi)
    cs = jnp.cos(ang_t).T
    sn = jnp.sin(ang_t).T
    first = lax.broadcasted_iota(jnp.int32, (half_tm, LANES), 1) < hq
    cs_sw = pltpu.roll(cs, hq, 1)
    sn_sw = pltpu.roll(sn, hq, 1)
    cos2 = jnp.concatenate([jnp.where(first, cs, cs_sw), jnp.where(first, cs_sw, cs)], axis=0)
    sin2 = jnp.concatenate([jnp.where(first, -sn, sn_sw), jnp.where(first, -sn_sw, sn)], axis=0)
    kscale = RET_QK_DIM ** -0.5
    for h in range(RET_HEADS):
        sl = slice(h * RET_QK_DIM, (h + 1) * RET_QK_DIM)
        qh = q[:, sl]
        kh = k[:, sl]
        q_scr[:, sl] = (qh * cos2 + pltpu.roll(qh, hq, 1) * sin2).astype(BF16)
        k_scr[:, sl] = (kh * cos2 + pltpu.roll(kh, hq, 1) * sin2) * kscale

    for g in range(SSM_GROUPS):
        s_g = _dot(uc_scr[g], wst_ref[g])
        s_scr[g * pitch:g * pitch + rows, :] = s_g
        ssw_scr[g * pitch:g * pitch + rows, :] = pltpu.roll(s_g, SSM_STATE, 1)

    v_scr[...] = proj(o_v, vw).astype(BF16)
    g_ret = proj(o_gr, vw)
    sg_scr[...] = (g_ret * _sigmoid(g_ret)).astype(BF16)

    gate_a = _sigmoid(proj(o_ga, d_model))
    gate_b = _sigmoid(proj(o_gb, d_model))

    lam_a = lama_ref[...]
    lam_b = lamb_ref[...]
    v = sv[...]
    w = sw[...]
    for n in range(rows):
        rsel = pl.ds(n, SSM_GROUPS, stride=pitch)
        xin_scr[rsel, :] = v
        v, w = (lam_a * v + lam_b * w + s_scr[rsel, :],
                lam_a * w - lam_b * v + ssw_scr[rsel, :])
    sv[...] = v
    sw[...] = w

    for g in range(SSM_GROUPS):
        x_in = xin_scr[g * pitch:g * pitch + rows, :].astype(BF16)
        yc_scr[g] = _dot(uc_scr[g], m_ref[g]) + _dot(x_in, cout_ref[g])

    for c in range(n_chunks):
        rs = slice(c * RET_CHUNK, (c + 1) * RET_CHUNK)
        qsl = [slice(h * RET_QK_DIM, (h + 1) * RET_QK_DIM) for h in range(RET_HEADS)]
        vsl = [slice(h * RET_V_DIM, (h + 1) * RET_V_DIM) for h in range(RET_HEADS)]
        qh = [q_scr[rs, qsl[h]] for h in range(RET_HEADS)]
        kf = [k_scr[rs, qsl[h]] for h in range(RET_HEADS)]
        vh = [v_scr[rs, vsl[h]] for h in range(RET_HEADS)]
        scores = [_dot_nt(qh[h], kf[h].astype(BF16)) * dec_ref[h] for h in range(RET_HEADS)]
        st = [rstate[h] for h in range(RET_HEADS)]
        cross = [_dot(qh[h], st[h].astype(BF16)) * xi_ref[h] for h in range(RET_HEADS)]
        for h in range(RET_HEADS):
            kz_t = (kf[h] * zeta_ref[h]).T.astype(BF16)
            rstate[h] = cd_ref[h] * st[h] + _dot(kz_t, vh[h])
        for h in range(RET_HEADS):
            ret = _dot(scores[h].astype(BF16), vh[h]) + cross[h]
            dev = ret - jnp.mean(ret, axis=-1, keepdims=True)
            rn = dev * lax.rsqrt(jnp.mean(dev * dev, axis=-1, keepdims=True) + EPS)
            gated_scr[rs, vsl[h]] = (rn * sg_scr[rs, vsl[h]].astype(F32)).astype(BF16)

    for qt in range(lane_tiles):
        for half in range(halves):
            pieces = [yc_scr[qt * GROUPS_PER_LANE_TILE + gl, :, half * LANES:(half + 1) * LANES]
                      for gl in range(GROUPS_PER_LANE_TILE)]
            t = _transpose8(pieces, lane_block)
            for i in range(8):
                ynat_scr[qt, pl.ds(8 * half + i, rows, stride=SSM_CHUNK), :] = t[i]
    z = jnp.concatenate([jax.nn.gelu(ynat_scr[qt]).astype(BF16) for qt in range(lane_tiles)], axis=1)

    glu = _dot(z, wglu_ref[...])
    y_b = glu[:, :d_model] * _sigmoid(glu[:, d_model:])
    y_a = _dot(gated_scr[...], wret_ref[...])

    mix = (gate_a * y_a + gate_b * y_b).astype(BF16)
    for rs in row_halves:
        mixed = _dot(mix[rs, :], wout_ref[...])
        o_ref[0, tile_rows(rs), :] = x_ref[0, tile_rows(rs), :] + mixed * _rms_scale(mixed) * gpost_ref[...]


def _ffn_kernel(h_ref, gpre_ref, wg_ref, wu_ref, wd_ref, gpost_ref, o_ref, *, hidden_chunks, row_block):
    for r0 in range(0, h_ref.shape[0], row_block):
        rs = slice(r0, r0 + row_block)
        h = h_ref[rs, :]
        f_in = (h * _rms_scale(h) * gpre_ref[...]).astype(BF16)
        f = None
        for lo, hi in hidden_chunks:
            gate = _dot(f_in, wg_ref[:, lo:hi])
            up = _dot(f_in, wu_ref[:, lo:hi])
            part = _dot((gate * _sigmoid(gate) * up).astype(BF16), wd_ref[lo:hi, :])
            f = part if f is None else f + part
        o_ref[rs, :] = h_ref[rs, :] + f * _rms_scale(f) * gpost_ref[...]


def _resident(shape):
    nd = len(shape)
    return pl.BlockSpec(shape, lambda *_: (0,) * nd, pipeline_mode=pl.Buffered(1))


def _retention_constants():
    c = RET_CHUNK
    log_g = jnp.log(1.0 - 2.0 ** (-5.0 - jnp.arange(RET_HEADS, dtype=F32)))
    idx = jnp.arange(c, dtype=F32)
    diff = idx[:, None] - idx[None, :]
    inner_decay = jnp.where(diff >= 0, jnp.exp(log_g[:, None, None] * jnp.maximum(diff, 0.0)), 0.0)
    zeta = jnp.exp(log_g[:, None] * (c - 1.0 - idx)[None, :])
    xi = jnp.exp(log_g[:, None] * (idx + 1.0)[None, :])
    chunk_decay = jnp.exp(log_g * c)
    zeta_full = jnp.broadcast_to(zeta[:, :, None], (RET_HEADS, c, RET_QK_DIM))
    xi_full = jnp.broadcast_to(xi[:, :, None], (RET_HEADS, c, RET_V_DIM))
    cd_full = jnp.broadcast_to(chunk_decay[:, None, None], (RET_HEADS, 1, RET_V_DIM))
    return inner_decay.astype(F32), xi_full.astype(F32), zeta_full.astype(F32), cd_full.astype(F32)


def _dot_f32(a, b):
    return jnp.dot(a, b, precision=lax.Precision.HIGHEST, preferred_element_type=F32)


def _s5_prep_kernel(arc_ref, aic_ref, arr_ref, air_ref, ldt_ref, btr_ref, bti_ref, ctr_ref, cti_ref,
                    dsk_ref, m_ref, wst_ref, cout_ref, lama_ref, lamb_ref):
    tc = SSM_CHUNK
    dt = jnp.exp(ldt_ref[...])

    def discretise(a_r, a_i):
        da_r = dt * a_r
        da_i = dt * a_i
        mag = jnp.exp(da_r)
        l_r = mag * jnp.cos(da_i)
        l_i = mag * jnp.sin(da_i)
        den = a_r * a_r + a_i * a_i
        num_r = l_r - 1.0
        return l_r, l_i, (num_r * a_r + l_i * a_i) / den, (l_i * a_r - num_r * a_i) / den

    lc_r, lc_i, _, _ = discretise(arc_ref[...], aic_ref[...])
    lr_r, lr_i, f_r, f_i = discretise(arr_ref[...], air_ref[...])
    bbt_r = f_r * btr_ref[...] - f_i * bti_ref[...]
    bbt_i = f_r * bti_ref[...] + f_i * btr_ref[...]

    lane_d = lax.broadcasted_iota(jnp.int32, (SSM_STATE, SSM_CHUNK_W), 1) // SSM_GROUP_CH
    pc_r = jnp.ones((SSM_STATE, 1), F32)
    pc_i = jnp.zeros((SSM_STATE, 1), F32)
    pw_r = jnp.zeros((SSM_STATE, SSM_CHUNK_W), F32)
    pw_i = jnp.zeros((SSM_STATE, SSM_CHUNK_W), F32)
    for d in range(tc):
        sel = lane_d == d
        pw_r = jnp.where(sel, pc_r, pw_r)
        pw_i = jnp.where(sel, pc_i, pw_i)
        pc_r, pc_i = pc_r * lc_r - pc_i * lc_i, pc_r * lc_i + pc_i * lc_r
    ct_r = ctr_ref[...]
    ct_i = cti_ref[...]

    r0 = (_dot_f32(bbt_r, pw_r * ct_r - pw_i * ct_i)
          - _dot_f32(bbt_i, pw_i * ct_r + pw_r * ct_i))
    row = lax.broadcasted_iota(jnp.int32, (SSM_GROUP_CH, SSM_CHUNK_W), 0)
    lane = lax.broadcasted_iota(jnp.int32, (SSM_GROUP_CH, SSM_CHUNK_W), 1)
    r0 = r0 + jnp.where(row == lane, dsk_ref[...], 0.0)
    for j in range(tc):
        blk = r0 if j == 0 else jnp.where(lane >= SSM_GROUP_CH * j, pltpu.roll(r0, SSM_GROUP_CH * j, 1), 0.0)
        m_ref[j * SSM_GROUP_CH:(j + 1) * SSM_GROUP_CH, :] = blk.astype(BF16)

    p1_r = pw_r * lc_r - pw_i * lc_i
    p1_i = pw_r * lc_i + pw_i * lc_r
    cout_ref[0:SSM_STATE, :] = (p1_r * ct_r - p1_i * ct_i).astype(BF16)
    cout_ref[SSM_STATE:2 * SSM_STATE, :] = (-(p1_i * ct_r + p1_r * ct_i)).astype(BF16)

    pr_r = jnp.ones((1, SSM_STATE), F32)
    pr_i = jnp.zeros((1, SSM_STATE), F32)
    powers = []
    for d in range(tc + 1):
        powers.append((pr_r, pr_i))
        pr_r, pr_i = pr_r * lr_r - pr_i * lr_i, pr_r * lr_i + pr_i * lr_r
    for j in range(tc):
        q_r, q_i = powers[tc - 1 - j]
        blk = jnp.concatenate([bbt_r * q_r - bbt_i * q_i, bbt_i * q_r + bbt_r * q_i], axis=1)
        wst_ref[j * SSM_GROUP_CH:(j + 1) * SSM_GROUP_CH, :] = blk.astype(BF16)
    q_r, q_i = powers[tc]
    lama_ref[...] = jnp.concatenate([q_r, q_r], axis=1)
    lamb_ref[...] = jnp.concatenate([-q_i, q_i], axis=1)


def _s5_prep_call(a_re, a_im, log_dt, b_re, b_im, c_re, c_im, d_skip):
    g, p, c = SSM_GROUPS, SSM_STATE, SSM_GROUP_CH
    tile_c = lambda m: jnp.tile(jnp.swapaxes(m, 1, 2), (1, 1, SSM_CHUNK))
    operands = [a_re.reshape(g, p, 1), a_im.reshape(g, p, 1), a_re.reshape(g, 1, p), a_im.reshape(g, 1, p),
                log_dt.reshape(g, 1, 1), jnp.swapaxes(b_re, 1, 2), jnp.swapaxes(b_im, 1, 2),
                tile_c(c_re), tile_c(c_im), jnp.tile(d_skip.reshape(g, 1, c), (1, 1, SSM_CHUNK))]
    out_shapes = [((SSM_CHUNK_W, SSM_CHUNK_W), BF16), ((SSM_CHUNK_W, 2 * p), BF16), ((2 * p, SSM_CHUNK_W), BF16),
                  ((1, 2 * p), F32), ((1, 2 * p), F32)]
    spec = lambda shp: pl.BlockSpec((None,) + tuple(shp), lambda i: (i, 0, 0))
    m, wst, cout, lam_a, lam_b = pl.pallas_call(
        _s5_prep_kernel,
        grid=(g,),
        in_specs=[spec(o.shape[1:]) for o in operands],
        out_specs=[spec(shp) for shp, _ in out_shapes],
        out_shape=[jax.ShapeDtypeStruct((g,) + shp, dt) for shp, dt in out_shapes],
        compiler_params=pltpu.CompilerParams(dimension_semantics=("arbitrary",)),
        name="s5_prep",
    )(*operands)
    return m, wst, cout, lam_a.reshape(g, 2 * p), lam_b.reshape(g, 2 * p)


def _mixer_call(x, pos_f, invf, gpre, win, ret_consts, wret, s5_ops, wglu, wout, gpost):
    bsz, seq, d_model = x.shape
    tile = min(MIXER_TILE, seq)
    tm = min(MIXER_SUB_TILE, tile)
    assert seq % tile == 0 and tile % tm == 0 and tm % RET_CHUNK == 0
    rows = tm // SSM_CHUNK
    pitch = _state_row_pitch(rows)
    inner_decay, xi_full, zeta_full, cd_full = ret_consts
    m, wst, cout, lam_a, lam_b = s5_ops
    qkw = RET_HEADS * RET_QK_DIM
    vw = RET_HEADS * RET_V_DIM
    ssm_w = SSM_GROUPS * SSM_GROUP_CH
    consts = [invf, gpre, win, inner_decay, xi_full, zeta_full, cd_full, wret, m, wst, cout,
              lam_a, lam_b, wglu, wout, gpost]
    in_specs = [pl.BlockSpec((1, tile, d_model), lambda b, t: (b, t, 0)),
                pl.BlockSpec((1, 1, tile // LANES, LANES), lambda b, t: (b, t, 0, 0))]
    in_specs += [_resident(c.shape) for c in consts]
    scratch = [
        pltpu.VMEM((RET_HEADS, RET_QK_DIM, RET_V_DIM), F32),
        pltpu.VMEM((SSM_GROUPS, 2 * SSM_STATE), F32),
        pltpu.VMEM((SSM_GROUPS, 2 * SSM_STATE), F32),
        pltpu.VMEM((tm, d_model), BF16),
        pltpu.VMEM((tm, qkw), BF16),
        pltpu.VMEM((tm, qkw), F32),
        pltpu.VMEM((tm, vw), BF16),
        pltpu.VMEM((tm, vw), BF16),
        pltpu.VMEM((tm, vw), BF16),
        pltpu.VMEM((ssm_w // LANES, tm, LANES), F32),
        pltpu.VMEM((SSM_GROUPS, rows, SSM_CHUNK_W), BF16),
        pltpu.VMEM((SSM_GROUPS * pitch, 2 * SSM_STATE), F32),
        pltpu.VMEM((SSM_GROUPS * pitch, 2 * SSM_STATE), F32),
        pltpu.VMEM((SSM_GROUPS * pitch, 2 * SSM_STATE), F32),
        pltpu.VMEM((SSM_GROUPS, rows, SSM_CHUNK_W), F32),
        pltpu.VMEM((ssm_w // LANES, tm, LANES), F32),
    ]
    return pl.pallas_call(
        functools.partial(_mixer_kernel, tm=tm),
        grid=(bsz, seq // tile),
        in_specs=in_specs,
        out_specs=pl.BlockSpec((1, tile, d_model), lambda b, t: (b, t, 0)),
        out_shape=jax.ShapeDtypeStruct(x.shape, F32),
        scratch_shapes=scratch,
        compiler_params=pltpu.CompilerParams(
            dimension_semantics=("arbitrary", "arbitrary"),
            vmem_limit_bytes=VMEM_LIMIT_BYTES),
        name="mixer",
    )(x, pos_f.reshape(bsz, seq // tile, tile // LANES, LANES), *consts)


def _ffn_call(h, gpre, wg, wu, wd, gpost):
    n_tok, d_model = h.shape
    hidden = wg.shape[1]
    tm = min(FFN_TILE, n_tok)
    assert n_tok % tm == 0
    step = 1024
    hidden_chunks = tuple((lo, min(lo + step, hidden)) for lo in range(0, hidden, step))
    consts = [gpre, wg, wu, wd, gpost]
    return pl.pallas_call(
        functools.partial(_ffn_kernel, hidden_chunks=hidden_chunks, row_block=min(FFN_ROW_BLOCK, tm)),
        grid=(n_tok // tm,),
        in_specs=[pl.BlockSpec((tm, d_model), lambda i: (i, 0))] + [_resident(c.shape) for c in consts],
        out_specs=pl.BlockSpec((tm, d_model), lambda i: (i, 0)),
        out_shape=jax.ShapeDtypeStruct(h.shape, F32),
        compiler_params=pltpu.CompilerParams(
            dimension_semantics=("arbitrary",),
            vmem_limit_bytes=VMEM_LIMIT_BYTES),
        name="ffn",
    )(h, *consts)


def kernel(x, positions, mix_pre_norm, w_in, ssm_a_re, ssm_a_im, ssm_log_dt, ssm_b_re, ssm_b_im, ssm_c_re, ssm_c_im, ssm_d, w_glu_val, w_glu_gate, w_ret_up, w_out, mix_post_norm, ffn_pre_norm, w_ffn_gate, w_ffn_up, w_ffn_down, ffn_post_norm):
    bsz, seq, d_model = x.shape
    depth = w_in.shape[0]
    pos_f = positions.astype(F32)
    inv_freq = ROPE_BASE ** (-jnp.arange(0, RET_QK_DIM, 2, dtype=F32) / RET_QK_DIM)
    invf = jnp.concatenate([inv_freq, inv_freq]).reshape(RET_QK_DIM, 1)
    ret_consts = _retention_constants()
    h = x
    for layer in range(depth):
        s5_ops = _s5_prep_call(ssm_a_re[layer], ssm_a_im[layer], ssm_log_dt[layer], ssm_b_re[layer],
                               ssm_b_im[layer], ssm_c_re[layer], ssm_c_im[layer], ssm_d[layer])
        wglu = jnp.concatenate([w_glu_val[layer], w_glu_gate[layer]], axis=1).astype(BF16)
        h = _mixer_call(h, pos_f, invf, mix_pre_norm[layer].reshape(1, d_model), w_in[layer].astype(BF16),
                        ret_consts, w_ret_up[layer].astype(BF16), s5_ops, wglu, w_out[layer].astype(BF16),
                        mix_post_norm[layer].reshape(1, d_model))
        h = _ffn_call(h.reshape(bsz * seq, d_model), ffn_pre_norm[layer].reshape(1, d_model),
                      w_ffn_gate[layer].astype(BF16), w_ffn_up[layer].astype(BF16),
                      w_ffn_down[layer].astype(BF16), ffn_post_norm[layer].reshape(1, d_model)
                      ).reshape(bsz, seq, d_model)
    return h
```

```python
import functools

import jax
import jax.numpy as jnp
from jax import lax
from jax.experimental import pallas as pl
from jax.experimental.pallas import tpu as pltpu

F32 = jnp.float32
BF16 = jnp.bfloat16

EPS = 1e-6
ROPE_BASE = 10000.0
RET_HEADS = 4
RET_QK_DIM = 128
RET_V_DIM = 256
RET_CHUNK = 128
SSM_GROUP_CH = 16
SSM_GROUPS = 32
SSM_STATE = 64
SSM_CHUNK = 8
SSM_CHUNK_W = SSM_CHUNK * SSM_GROUP_CH
LANES = 128
BF16_ROWS = 16
GROUPS_PER_LANE_TILE = LANES // SSM_GROUP_CH
VMEM_LIMIT_BYTES = 58 * 1024 * 1024

MIXER_TILE = 1024
MIXER_SUB_TILE = 512
FFN_TILE = 1024
FFN_ROW_BLOCK = 256


def _dot(a, b):
    return jnp.dot(a, b, preferred_element_type=F32)


def _dot_nt(a, b):
    return lax.dot_general(a, b, (((1,), (1,)), ((), ())), preferred_element_type=F32)


def _sigmoid(x):
    return 0.5 * jnp.tanh(0.5 * x) + 0.5


def _rms_scale(x):
    return lax.rsqrt(jnp.mean(x * x, axis=-1, keepdims=True) + EPS)


def _transpose8(pieces, lane_block):
    a = list(pieces)
    s = 4
    while s >= 1:
        keep = (lane_block & s) == 0
        shift = SSM_GROUP_CH * s
        for i in range(8):
            if i & s:
                continue
            lo, hi = a[i], a[i + s]
            a[i] = jnp.where(keep, lo, pltpu.roll(hi, shift, 1))
            a[i + s] = jnp.where(keep, pltpu.roll(lo, LANES - shift, 1), hi)
        s //= 2
    return a


def _state_row_pitch(rows):
    assert rows % 8 == 0
    return rows + 4


MIXER_INPUTS = 17
CAST_WEIGHTS = 2


def _mixer_kernel(*refs, tm):
    main_in = refs[:MIXER_INPUTS]
    cast_in = refs[MIXER_INPUTS:MIXER_INPUTS + CAST_WEIGHTS]
    o_ref = refs[MIXER_INPUTS + CAST_WEIGHTS]
    cast_out = refs[MIXER_INPUTS + CAST_WEIGHTS + 1:MIXER_INPUTS + 2 * CAST_WEIGHTS + 1]
    scratch = refs[MIXER_INPUTS + 2 * CAST_WEIGHTS + 1:]
    rstate, sv, sw = scratch[:3]

    @pl.when(pl.program_id(1) == 0)
    def _():
        rstate[...] = jnp.zeros_like(rstate)
        sv[...] = jnp.zeros_like(sv)
        sw[...] = jnp.zeros_like(sw)

    for src, dst in zip(cast_in, cast_out):
        dst[...] = src[...].astype(BF16)
    for base in range(0, main_in[0].shape[1], tm):
        _mixer_sub_tile(base, tm, *main_in, o_ref, *scratch)


def _mixer_sub_tile(base, tm, x_ref, pos_ref, invf_ref, gpre_ref, win_ref, dec_ref, xi_ref, zeta_ref, cd_ref,
                    wret_ref, msw_ref, cout_ref, lama_ref, lamb_ref, wglu_ref, wout_ref,
                    gpost_ref, o_ref,
                    rstate, sv, sw, u_scr, q_scr, k_scr, v_scr, sg_scr, gated_scr, ussm_scr, uc_scr,
                    s_scr, ssw_scr, xin_scr, yc_scr):
    ynat_scr = ussm_scr
    n_chunks = tm // RET_CHUNK
    rows = tm // SSM_CHUNK
    pitch = _state_row_pitch(rows)
    qkw = RET_HEADS * RET_QK_DIM
    vw = RET_HEADS * RET_V_DIM
    ssm_w = SSM_GROUPS * SSM_GROUP_CH
    lane_tiles = ssm_w // LANES
    halves = SSM_CHUNK // 8
    d_model = x_ref.shape[2]
    hq = RET_QK_DIM // 2
    o_v = 2 * qkw
    o_gr = o_v + vw
    o_ssm = o_gr + vw
    o_ga = o_ssm + ssm_w
    o_gb = o_ga + d_model

    def tile_rows(rs):
        return slice(base + rs.start, base + rs.stop)

    def proj(lo, width):
        return _dot(u_scr[...], win_ref[:, lo:lo + width])

    lane_block = lax.broadcasted_iota(jnp.int32, (rows, LANES), 1) // SSM_GROUP_CH


    row_halves = [slice(0, tm // 2), slice(tm // 2, tm)]
    for rs in row_halves:
        x = x_ref[0, tile_rows(rs), :]
        u_scr[rs, :] = (x * _rms_scale(x) * gpre_ref[...]).astype(BF16)

    for rs in row_halves:
        u_ssm = _dot(u_scr[rs, :], win_ref[:, o_ssm:o_ssm + ssm_w])
        for qt in range(lane_tiles):
            ussm_scr[qt, rs, :] = u_ssm[:, qt * LANES:(qt + 1) * LANES]

    for qt in range(lane_tiles):
        for half in range(halves):
            pieces = [ussm_scr[qt, pl.ds(8 * half + j, rows, stride=SSM_CHUNK), :] for j in range(8)]
            t = _transpose8(pieces, lane_block)
            for gl in range(GROUPS_PER_LANE_TILE):
                uc_scr[qt * GROUPS_PER_LANE_TILE + gl, :, half * LANES:(half + 1) * LANES] = t[gl].astype(BF16)

    half_tm = tm // 2
    pos_rows = pos_ref[0, 0, base // LANES:(base + tm) // LANES, :]
    half_rows = half_tm // LANES
    pos_lo = jnp.concatenate([pos_rows[r:r + 1, :] for r in range(half_rows)], axis=1)
    pos_hi = jnp.concatenate([pos_rows[r:r + 1, :] for r in range(half_rows, 2 * half_rows)], axis=1)
    upper = lax.broadcasted_iota(jnp.int32, (LANES, half_tm), 0) < hq
    ang_t = invf_ref[...] * jnp.where(upper, pos_lo, pos_hi)
    cs = jnp.cos(ang_t).T
    sn = jnp.sin(ang_t).T
    first = lax.broadcasted_iota(jnp.int32, (half_tm, LANES), 1) < hq
    cs_sw = pltpu.roll(cs, hq, 1)
    sn_sw = pltpu.roll(sn, hq, 1)
    cos2 = jnp.concatenate([jnp.where(first, cs, cs_sw), jnp.where(first, cs_sw, cs)], axis=0)
    sin2 = jnp.concatenate([jnp.where(first, -sn, sn_sw), jnp.where(first, -sn_sw, sn)], axis=0)

    v_scr[...] = proj(o_v, vw).astype(BF16)
    g_ret = proj(o_gr, vw)
    sg_scr[...] = (g_ret * _sigmoid(g_ret)).astype(BF16)

    q = proj(0, qkw)
    k = proj(qkw, qkw)
    kscale = RET_QK_DIM ** -0.5
    for h in range(RET_HEADS):
        sl = slice(h * RET_QK_DIM, (h + 1) * RET_QK_DIM)
        qh = q[:, sl]
        kh = k[:, sl]
        q_scr[:, sl] = (qh * cos2 + pltpu.roll(qh, hq, 1) * sin2).astype(BF16)
        k_scr[:, sl] = (kh * cos2 + pltpu.roll(kh, hq, 1) * sin2) * kscale

    for g in range(SSM_GROUPS):
        sy = _dot(uc_scr[g], msw_ref[g])
        yc_scr[g] = sy[:, :SSM_CHUNK_W]
        s_g = sy[:, SSM_CHUNK_W:]
        s_scr[g * pitch:g * pitch + rows, :] = s_g
        ssw_scr[g * pitch:g * pitch + rows, :] = pltpu.roll(s_g, SSM_STATE, 1)

    gate_a = _sigmoid(proj(o_ga, d_model))
    gate_b = _sigmoid(proj(o_gb, d_model))

    lam_a = lama_ref[...]
    lam_b = lamb_ref[...]
    v = sv[...]
    w = sw[...]
    for n in range(rows):
        rsel = pl.ds(n, SSM_GROUPS, stride=pitch)
        xin_scr[rsel, :] = v
        v, w = (lam_a * v + lam_b * w + s_scr[rsel, :],
                lam_a * w - lam_b * v + ssw_scr[rsel, :])
    sv[...] = v
    sw[...] = w

    for g in range(SSM_GROUPS):
        x_in = xin_scr[g * pitch:g * pitch + rows, :].astype(BF16)
        yc_scr[g] += _dot(x_in, cout_ref[g])

    qsl = [slice(h * RET_QK_DIM, (h + 1) * RET_QK_DIM) for h in range(RET_HEADS)]
    vsl = [slice(h * RET_V_DIM, (h + 1) * RET_V_DIM) for h in range(RET_HEADS)]

    def chunk_head(c):
        rs = slice(c * RET_CHUNK, (c + 1) * RET_CHUNK)
        qh = [q_scr[rs, qsl[h]] for h in range(RET_HEADS)]
        kf = [k_scr[rs, qsl[h]] for h in range(RET_HEADS)]
        vh = [v_scr[rs, vsl[h]] for h in range(RET_HEADS)]
        scores = [_dot_nt(qh[h], kf[h].astype(BF16)) * dec_ref[h] for h in range(RET_HEADS)]
        st = [rstate[h] for h in range(RET_HEADS)]
        cross = [_dot(qh[h], st[h].astype(BF16)) * xi_ref[h] for h in range(RET_HEADS)]
        for h in range(RET_HEADS):
            kz_t = (kf[h] * zeta_ref[h]).T.astype(BF16)
            rstate[h] = cd_ref[h] * st[h] + _dot(kz_t, vh[h])
        return rs, scores, cross, vh

    def chunk_tail(rs, scores, cross, vh):
        for h in range(RET_HEADS):
            ret = _dot(scores[h].astype(BF16), vh[h]) + cross[h]
            dev = ret - jnp.mean(ret, axis=-1, keepdims=True)
            rn = dev * lax.rsqrt(jnp.mean(dev * dev, axis=-1, keepdims=True) + EPS)
            gated_scr[rs, vsl[h]] = rn.astype(BF16) * sg_scr[rs, vsl[h]]

    pending = chunk_head(0)
    for c in range(1, n_chunks):
        following = chunk_head(c)
        chunk_tail(*pending)
        pending = following
    chunk_tail(*pending)

    for qt in range(lane_tiles):
        for half in range(halves):
            pieces = [yc_scr[qt * GROUPS_PER_LANE_TILE + gl, :, half * LANES:(half + 1) * LANES]
                      for gl in range(GROUPS_PER_LANE_TILE)]
            t = _transpose8(pieces, lane_block)
            for i in range(8):
                ynat_scr[qt, pl.ds(8 * half + i, rows, stride=SSM_CHUNK), :] = t[i]
    z = jnp.concatenate([jax.nn.gelu(ynat_scr[qt]).astype(BF16) for qt in range(lane_tiles)], axis=1)

    glu = _dot(z, wglu_ref[...])
    y_b = glu[:, :d_model] * _sigmoid(glu[:, d_model:])
    y_a = _dot(gated_scr[...], wret_ref[...])

    mix = (gate_a * y_a + gate_b * y_b).astype(BF16)
    for rs in row_halves:
        mixed = _dot(mix[rs, :], wout_ref[...])
        o_ref[0, tile_rows(rs), :] = x_ref[0, tile_rows(rs), :] + mixed * _rms_scale(mixed) * gpost_ref[...]


def _ffn_kernel(h_ref, gpre_ref, wg_ref, wu_ref, wd_ref, gpost_ref, o_ref, *, hidden_chunks, row_block):
    for r0 in range(0, h_ref.shape[0], row_block):
        rs = slice(r0, r0 + row_block)
        h = h_ref[rs, :]
        f_in = (h * _rms_scale(h) * gpre_ref[...]).astype(BF16)
        f = None
        for lo, hi in hidden_chunks:
            gate = _dot(f_in, wg_ref[:, lo:hi])
            up = _dot(f_in, wu_ref[:, lo:hi])
            part = _dot((gate * _sigmoid(gate) * up).astype(BF16), wd_ref[lo:hi, :])
            f = part if f is None else f + part
        o_ref[rs, :] = h_ref[rs, :] + f * _rms_scale(f) * gpost_ref[...]


def _resident(shape):
    nd = len(shape)
    return pl.BlockSpec(shape, lambda *_: (0,) * nd, pipeline_mode=pl.Buffered(1))


def _retention_constants():
    c = RET_CHUNK
    log_g = jnp.log(1.0 - 2.0 ** (-5.0 - jnp.arange(RET_HEADS, dtype=F32)))
    idx = jnp.arange(c, dtype=F32)
    diff = idx[:, None] - idx[None, :]
    inner_decay = jnp.where(diff >= 0, jnp.exp(log_g[:, None, None] * jnp.maximum(diff, 0.0)), 0.0)
    zeta = jnp.exp(log_g[:, None] * (c - 1.0 - idx)[None, :])
    xi = jnp.exp(log_g[:, None] * (idx + 1.0)[None, :])
    chunk_decay = jnp.exp(log_g * c)
    zeta_full = jnp.broadcast_to(zeta[:, :, None], (RET_HEADS, c, RET_QK_DIM))
    xi_full = jnp.broadcast_to(xi[:, :, None], (RET_HEADS, c, RET_V_DIM))
    cd_full = jnp.broadcast_to(chunk_decay[:, None, None], (RET_HEADS, 1, RET_V_DIM))
    return inner_decay.astype(F32), xi_full.astype(F32), zeta_full.astype(F32), cd_full.astype(F32)


def _dot_f32(a, b):
    return jnp.dot(a, b, precision=lax.Precision.HIGHEST, preferred_element_type=F32)


def _s5_prep_kernel(arc_ref, aic_ref, arr_ref, air_ref, ldt_ref, btr_ref, bti_ref, ctr_ref, cti_ref,
                    dsk_ref, win_f, wret_f, wout_f, wfd_f, wgv_f, wgg_f,
                    msw_ref, cout_ref, lama_ref, lamb_ref, win_o, wret_o, wout_o, wfd_o, wglu_o):
    win_o[...] = win_f[...].astype(BF16)
    wret_o[...] = wret_f[...].astype(BF16)
    wout_o[...] = wout_f[...].astype(BF16)
    wfd_o[...] = wfd_f[...].astype(BF16)
    glu_w = wgv_f.shape[1]
    wglu_o[:, 0:glu_w] = wgv_f[...].astype(BF16)
    wglu_o[:, glu_w:2 * glu_w] = wgg_f[...].astype(BF16)

    tc = SSM_CHUNK
    dt = jnp.exp(ldt_ref[...])

    def discretise(a_r, a_i):
        da_r = dt * a_r
        da_i = dt * a_i
        mag = jnp.exp(da_r)
        l_r = mag * jnp.cos(da_i)
        l_i = mag * jnp.sin(da_i)
        den = a_r * a_r + a_i * a_i
        num_r = l_r - 1.0
        return l_r, l_i, (num_r * a_r + l_i * a_i) / den, (l_i * a_r - num_r * a_i) / den

    lc_r, lc_i, _, _ = discretise(arc_ref[...], aic_ref[...])
    lr_r, lr_i, f_r, f_i = discretise(arr_ref[...], air_ref[...])
    bbt_r = f_r * btr_ref[...] - f_i * bti_ref[...]
    bbt_i = f_r * bti_ref[...] + f_i * btr_ref[...]

    lane_d = lax.broadcasted_iota(jnp.int32, (SSM_STATE, SSM_CHUNK_W), 1) // SSM_GROUP_CH
    pc_r = jnp.ones((SSM_STATE, 1), F32)
    pc_i = jnp.zeros((SSM_STATE, 1), F32)
    pw_r = jnp.zeros((SSM_STATE, SSM_CHUNK_W), F32)
    pw_i = jnp.zeros((SSM_STATE, SSM_CHUNK_W), F32)
    for d in range(tc):
        sel = lane_d == d
        pw_r = jnp.where(sel, pc_r, pw_r)
        pw_i = jnp.where(sel, pc_i, pw_i)
        pc_r, pc_i = pc_r * lc_r - pc_i * lc_i, pc_r * lc_i + pc_i * lc_r
    ct_r = ctr_ref[...]
    ct_i = cti_ref[...]

    r0 = (_dot_f32(bbt_r, pw_r * ct_r - pw_i * ct_i)
          - _dot_f32(bbt_i, pw_i * ct_r + pw_r * ct_i))
    row = lax.broadcasted_iota(jnp.int32, (SSM_GROUP_CH, SSM_CHUNK_W), 0)
    lane = lax.broadcasted_iota(jnp.int32, (SSM_GROUP_CH, SSM_CHUNK_W), 1)
    r0 = r0 + jnp.where(row == lane, dsk_ref[...], 0.0)
    for j in range(tc):
        blk = r0 if j == 0 else jnp.where(lane >= SSM_GROUP_CH * j, pltpu.roll(r0, SSM_GROUP_CH * j, 1), 0.0)
        msw_ref[j * SSM_GROUP_CH:(j + 1) * SSM_GROUP_CH, 0:SSM_CHUNK_W] = blk.astype(BF16)

    p1_r = pw_r * lc_r - pw_i * lc_i
    p1_i = pw_r * lc_i + pw_i * lc_r
    cout_ref[0:SSM_STATE, :] = (p1_r * ct_r - p1_i * ct_i).astype(BF16)
    cout_ref[SSM_STATE:2 * SSM_STATE, :] = (-(p1_i * ct_r + p1_r * ct_i)).astype(BF16)

    pr_r = jnp.ones((1, SSM_STATE), F32)
    pr_i = jnp.zeros((1, SSM_STATE), F32)
    powers = []
    for d in range(tc + 1):
        powers.append((pr_r, pr_i))
        pr_r, pr_i = pr_r * lr_r - pr_i * lr_i, pr_r * lr_i + pr_i * lr_r
    for j in range(tc):
        q_r, q_i = powers[tc - 1 - j]
        blk = jnp.concatenate([bbt_r * q_r - bbt_i * q_i, bbt_i * q_r + bbt_r * q_i], axis=1)
        msw_ref[j * SSM_GROUP_CH:(j + 1) * SSM_GROUP_CH, SSM_CHUNK_W:] = blk.astype(BF16)
    q_r, q_i = powers[tc]
    lama_ref[...] = jnp.concatenate([q_r, q_r], axis=1)
    lamb_ref[...] = jnp.concatenate([-q_i, q_i], axis=1)


def _s5_prep_call(a_re, a_im, log_dt, b_re, b_im, c_re, c_im, d_skip,
                  w_in, w_ret_up, w_out, w_ffn_down, w_glu_val, w_glu_gate):
    g, p, c = SSM_GROUPS, SSM_STATE, SSM_GROUP_CH
    tile_c = lambda m: jnp.tile(jnp.swapaxes(m, 1, 2), (1, 1, SSM_CHUNK))
    operands = [a_re.reshape(g, p, 1), a_im.reshape(g, p, 1), a_re.reshape(g, 1, p), a_im.reshape(g, 1, p),
                log_dt.reshape(g, 1, 1), jnp.swapaxes(b_re, 1, 2), jnp.swapaxes(b_im, 1, 2),
                tile_c(c_re), tile_c(c_im), jnp.tile(d_skip.reshape(g, 1, c), (1, 1, SSM_CHUNK))]
    out_shapes = [((SSM_CHUNK_W, SSM_CHUNK_W + 2 * p), BF16), ((2 * p, SSM_CHUNK_W), BF16),
                  ((1, 2 * p), F32), ((1, 2 * p), F32)]
    spec = lambda shp: pl.BlockSpec((None,) + tuple(shp), lambda i: (i, 0, 0))

    def row_spec(n_rows, n_cols):
        rows_blk, n_blocks = _row_blocks(n_rows, g)
        return pl.BlockSpec((rows_blk, n_cols), lambda i, last=n_blocks - 1: (jnp.minimum(i, last), 0))

    weights = [w_in, w_ret_up, w_out, w_ffn_down, w_glu_val, w_glu_gate]
    glu_shape = (w_glu_val.shape[0], w_glu_val.shape[1] + w_glu_gate.shape[1])
    cast_shapes = [w_in.shape, w_ret_up.shape, w_out.shape, w_ffn_down.shape, glu_shape]
    msw, cout, lam_a, lam_b, win, wret, wout, wfd, wglu = pl.pallas_call(
        _s5_prep_kernel,
        grid=(g,),
        in_specs=[spec(o.shape[1:]) for o in operands] + [row_spec(*w.shape) for w in weights],
        out_specs=[spec(shp) for shp, _ in out_shapes] + [row_spec(*shp) for shp in cast_shapes],
        out_shape=([jax.ShapeDtypeStruct((g,) + shp, dt) for shp, dt in out_shapes]
                   + [jax.ShapeDtypeStruct(shp, BF16) for shp in cast_shapes]),
        compiler_params=pltpu.CompilerParams(dimension_semantics=("arbitrary",)),
        name="s5_prep",
    )(*operands, *weights)
    return (msw, cout, lam_a.reshape(g, 2 * p), lam_b.reshape(g, 2 * p)), (win, wret, wout, wglu, wfd)


def _row_blocks(n_rows, n_steps):
    units = n_rows // BF16_ROWS
    assert units * BF16_ROWS == n_rows
    n_blocks = max(d for d in range(1, min(units, n_steps) + 1) if units % d == 0)
    return n_rows // n_blocks, n_blocks


def _mixer_call(x, pos_f, invf, gpre, win, ret_consts, wret, s5_ops, wglu, wout, gpost, cast_weights):
    bsz, seq, d_model = x.shape
    tile = min(MIXER_TILE, seq)
    tm = min(MIXER_SUB_TILE, tile)
    assert seq % tile == 0 and tile % tm == 0 and tm % RET_CHUNK == 0
    rows = tm // SSM_CHUNK
    pitch = _state_row_pitch(rows)
    inner_decay, xi_full, zeta_full, cd_full = ret_consts
    msw, cout, lam_a, lam_b = s5_ops
    qkw = RET_HEADS * RET_QK_DIM
    vw = RET_HEADS * RET_V_DIM
    ssm_w = SSM_GROUPS * SSM_GROUP_CH
    consts = [invf, gpre, win, inner_decay, xi_full, zeta_full, cd_full, wret, msw, cout,
              lam_a, lam_b, wglu, wout, gpost]
    in_specs = [pl.BlockSpec((1, tile, d_model), lambda b, t: (b, t, 0)),
                pl.BlockSpec((1, 1, tile // LANES, LANES), lambda b, t: (b, t, 0, 0))]
    in_specs += [_resident(c.shape) for c in consts]
    assert len(in_specs) == MIXER_INPUTS and len(cast_weights) == CAST_WEIGHTS
    n_t = seq // tile
    cast_specs = []
    for w in cast_weights:
        rows_blk, n_blocks = _row_blocks(w.shape[0], bsz * n_t)
        cast_specs.append(pl.BlockSpec(
            (rows_blk, w.shape[1]), lambda b, t, last=n_blocks - 1: (jnp.minimum(b * n_t + t, last), 0)))
    scratch = [
        pltpu.VMEM((RET_HEADS, RET_QK_DIM, RET_V_DIM), F32),
        pltpu.VMEM((SSM_GROUPS, 2 * SSM_STATE), F32),
        pltpu.VMEM((SSM_GROUPS, 2 * SSM_STATE), F32),
        pltpu.VMEM((tm, d_model), BF16),
        pltpu.VMEM((tm, qkw), BF16),
        pltpu.VMEM((tm, qkw), F32),
        pltpu.VMEM((tm, vw), BF16),
        pltpu.VMEM((tm, vw), BF16),
        pltpu.VMEM((tm, vw), BF16),
        pltpu.VMEM((ssm_w // LANES, tm, LANES), F32),
        pltpu.VMEM((SSM_GROUPS, rows, SSM_CHUNK_W), BF16),
        pltpu.VMEM((SSM_GROUPS * pitch, 2 * SSM_STATE), F32),
        pltpu.VMEM((SSM_GROUPS * pitch, 2 * SSM_STATE), F32),
        pltpu.VMEM((SSM_GROUPS * pitch, 2 * SSM_STATE), F32),
        pltpu.VMEM((SSM_GROUPS, rows, SSM_CHUNK_W), F32),
    ]
    out, *cast = pl.pallas_call(
        functools.partial(_mixer_kernel, tm=tm),
        grid=(bsz, n_t),
        in_specs=in_specs + cast_specs,
        out_specs=[pl.BlockSpec((1, tile, d_model), lambda b, t: (b, t, 0))] + cast_specs,
        out_shape=[jax.ShapeDtypeStruct(x.shape, F32)] + [jax.ShapeDtypeStruct(w.shape, BF16) for w in cast_weights],
        scratch_shapes=scratch,
        compiler_params=pltpu.CompilerParams(
            dimension_semantics=("arbitrary", "arbitrary"),
            vmem_limit_bytes=VMEM_LIMIT_BYTES),
        name="mixer",
    )(x, pos_f.reshape(bsz, n_t, tile // LANES, LANES), *consts, *cast_weights)
    return out, cast


def _ffn_call(h, gpre, wg, wu, wd, gpost):
    n_tok, d_model = h.shape
    hidden = wg.shape[1]
    tm = min(FFN_TILE, n_tok)
    assert n_tok % tm == 0
    step = 1024
    hidden_chunks = tuple((lo, min(lo + step, hidden)) for lo in range(0, hidden, step))
    consts = [gpre, wg, wu, wd, gpost]
    return pl.pallas_call(
        functools.partial(_ffn_kernel, hidden_chunks=hidden_chunks, row_block=min(FFN_ROW_BLOCK, tm)),
        grid=(n_tok // tm,),
        in_specs=[pl.BlockSpec((tm, d_model), lambda i: (i, 0))] + [_resident(c.shape) for c in consts],
        out_specs=pl.BlockSpec((tm, d_model), lambda i: (i, 0)),
        out_shape=jax.ShapeDtypeStruct(h.shape, F32),
        compiler_params=pltpu.CompilerParams(
            dimension_semantics=("arbitrary",),
            vmem_limit_bytes=VMEM_LIMIT_BYTES),
        name="ffn",
    )(h, *consts)


def kernel(x, positions, mix_pre_norm, w_in, ssm_a_re, ssm_a_im, ssm_log_dt, ssm_b_re, ssm_b_im, ssm_c_re, ssm_c_im, ssm_d, w_glu_val, w_glu_gate, w_ret_up, w_out, mix_post_norm, ffn_pre_norm, w_ffn_gate, w_ffn_up, w_ffn_down, ffn_post_norm):
    bsz, seq, d_model = x.shape
    depth = w_in.shape[0]
    pos_f = positions.astype(F32)
    inv_freq = ROPE_BASE ** (-jnp.arange(0, RET_QK_DIM, 2, dtype=F32) / RET_QK_DIM)
    invf = jnp.concatenate([inv_freq, inv_freq]).reshape(RET_QK_DIM, 1)
    ret_consts = _retention_constants()
    h = x
    for layer in range(depth):
        s5_ops, (win, wret, wout, wglu, wfd) = _s5_prep_call(
            ssm_a_re[layer], ssm_a_im[layer], ssm_log_dt[layer], ssm_b_re[layer], ssm_b_im[layer],
            ssm_c_re[layer], ssm_c_im[layer], ssm_d[layer],
            w_in[layer], w_ret_up[layer], w_out[layer], w_ffn_down[layer], w_glu_val[layer], w_glu_gate[layer])
        h, (wfg, wfu) = _mixer_call(
            h, pos_f, invf, mix_pre_norm[layer].reshape(1, d_model), win,
            ret_consts, wret, s5_ops, wglu, wout,
            mix_post_norm[layer].reshape(1, d_model),
            [w_ffn_gate[layer], w_ffn_up[layer]])
        h = _ffn_call(h.reshape(bsz * seq, d_model), ffn_pre_norm[layer].reshape(1, d_model),
                      wfg, wfu, wfd, ffn_post_norm[layer].reshape(1, d_model)).reshape(bsz, seq, d_model)
    return h
```

```python
import functools

import jax
import jax.numpy as jnp
from jax import lax
from jax.experimental import pallas as pl
from jax.experimental.pallas import tpu as pltpu

F32 = jnp.float32
BF16 = jnp.bfloat16

EPS = 1e-6
ROPE_BASE = 10000.0
RET_HEADS = 4
RET_QK_DIM = 128
RET_V_DIM = 256
RET_CHUNK = 128
SSM_GROUP_CH = 16
SSM_GROUPS = 32
SSM_STATE = 64
SSM_CHUNK = 8
SSM_CHUNK_W = SSM_CHUNK * SSM_GROUP_CH
LANES = 128
GROUPS_PER_LANE_TILE = LANES // SSM_GROUP_CH
VMEM_LIMIT_BYTES = 58 * 1024 * 1024

MIXER_TILE = 1024
MIXER_SUB_TILE = 512
FFN_TILE = 2048
FFN_ROW_BLOCK = 256
PREP_GROUPS_PER_STEP = 4


def _dot(a, b):
    return jnp.dot(a, b, preferred_element_type=F32)


def _dot_nt(a, b):
    return lax.dot_general(a, b, (((1,), (1,)), ((), ())), preferred_element_type=F32)


def _sigmoid(x):
    return 0.5 * jnp.tanh(0.5 * x) + 0.5


def _rms_scale(x):
    return lax.rsqrt(jnp.mean(x * x, axis=-1, keepdims=True) + EPS)


def _transpose8(pieces, lane_block):
    a = list(pieces)
    s = 4
    while s >= 1:
        keep = (lane_block & s) == 0
        shift = SSM_GROUP_CH * s
        for i in range(8):
            if i & s:
                continue
            lo, hi = a[i], a[i + s]
            a[i] = jnp.where(keep, lo, pltpu.roll(hi, shift, 1))
            a[i + s] = jnp.where(keep, pltpu.roll(lo, LANES - shift, 1), hi)
        s //= 2
    return a


def _state_row_pitch(rows):
    assert rows % 8 == 0
    return rows + 4


def _mixer_kernel(*refs, tm):
    x_ref = refs[0]
    rstate, sv, sw = refs[19:22]

    @pl.when(pl.program_id(1) == 0)
    def _():
        rstate[...] = jnp.zeros_like(rstate)
        sv[...] = jnp.zeros_like(sv)
        sw[...] = jnp.zeros_like(sw)

    for base in range(0, x_ref.shape[1], tm):
        _mixer_sub_tile(base, tm, *refs)


def _mixer_sub_tile(base, tm, x_ref, pos_ref, invf_ref, gpre_ref, win_ref, dec_ref, xi_ref, zeta_ref, cd_ref,
                    wret_ref, m_ref, wst_ref, cout_ref, lama_ref, lamb_ref, wglu_ref, wout_ref,
                    gpost_ref, o_ref,
                    rstate, sv, sw, u_scr, q_scr, k_scr, v_scr, sg_scr, gated_scr, ussm_scr, uc_scr,
                    s_scr, ssw_scr, xin_scr, yc_scr, ynat_scr):
    n_chunks = tm // RET_CHUNK
    rows = tm // SSM_CHUNK
    pitch = _state_row_pitch(rows)
    qkw = RET_HEADS * RET_QK_DIM
    vw = RET_HEADS * RET_V_DIM
    ssm_w = SSM_GROUPS * SSM_GROUP_CH
    lane_tiles = ssm_w // LANES
    halves = SSM_CHUNK // 8
    d_model = x_ref.shape[2]
    hq = RET_QK_DIM // 2
    o_v = 2 * qkw
    o_gr = o_v + vw
    o_ssm = o_gr + vw
    o_ga = o_ssm + ssm_w
    o_gb = o_ga + d_model

    def tile_rows(rs):
        return slice(base + rs.start, base + rs.stop)

    def proj(lo, width):
        return _dot(u_scr[...], win_ref[:, lo:lo + width])

    lane_block = lax.broadcasted_iota(jnp.int32, (rows, LANES), 1) // SSM_GROUP_CH


    row_halves = [slice(0, tm // 2), slice(tm // 2, tm)]
    for rs in row_halves:
        x = x_ref[0, tile_rows(rs), :]
        u_scr[rs, :] = (x * _rms_scale(x) * gpre_ref[...]).astype(BF16)

    for rs in row_halves:
        u_ssm = _dot(u_scr[rs, :], win_ref[:, o_ssm:o_ssm + ssm_w])
        for qt in range(lane_tiles):
            ussm_scr[qt, rs, :] = u_ssm[:, qt * LANES:(qt + 1) * LANES]

    q = proj(0, qkw)
    k = proj(qkw, qkw)

    for qt in range(lane_tiles):
        for half in range(halves):
            pieces = [ussm_scr[qt, pl.ds(8 * half + j, rows, stride=SSM_CHUNK), :] for j in range(8)]
            t = _transpose8(pieces, lane_block)
            for gl in range(GROUPS_PER_LANE_TILE):
                uc_scr[qt * GROUPS_PER_LANE_TILE + gl, :, half * LANES:(half + 1) * LANES] = t[gl].astype(BF16)

    half_tm = tm // 2
    pos_rows = pos_ref[0, pl.program_id(1), base // LANES:(base + tm) // LANES, :]
    half_rows = half_tm // LANES
    pos_lo = jnp.concatenate([pos_rows[r:r + 1, :] for r in range(half_rows)], axis=1)
    pos_hi = jnp.concatenate([pos_rows[r:r + 1, :] for r in range(half_rows, 2 * half_rows)], axis=1)
    upper = lax.broadcasted_iota(jnp.int32, (LANES, half_tm), 0) < hq
    ang_t = invf_ref[...] * jnp.where(upper, pos_lo, pos_hi)
    cs = jnp.cos(ang_t).T
    sn = jnp.sin(ang_t).T
    first = lax.broadcasted_iota(jnp.int32, (half_tm, LANES), 1) < hq
    cs_sw = pltpu.roll(cs, hq, 1)
    sn_sw = pltpu.roll(sn, hq, 1)
    cos2 = jnp.concatenate([jnp.where(first, cs, cs_sw), jnp.where(first, cs_sw, cs)], axis=0)
    sin2 = jnp.concatenate([jnp.where(first, -sn, sn_sw), jnp.where(first, -sn_sw, sn)], axis=0)
    kscale = RET_QK_DIM ** -0.5
    for h in range(RET_HEADS):
        sl = slice(h * RET_QK_DIM, (h + 1) * RET_QK_DIM)
        qh = q[:, sl]
        kh = k[:, sl]
        q_scr[:, sl] = (qh * cos2 + pltpu.roll(qh, hq, 1) * sin2).astype(BF16)
        k_scr[:, sl] = (kh * cos2 + pltpu.roll(kh, hq, 1) * sin2) * kscale

    for g in range(SSM_GROUPS):
        s_g = _dot(uc_scr[g], wst_ref[g])
        s_scr[g * pitch:g * pitch + rows, :] = s_g
        ssw_scr[g * pitch:g * pitch + rows, :] = pltpu.roll(s_g, SSM_STATE, 1)

    v_scr[...] = proj(o_v, vw).astype(BF16)
    g_ret = proj(o_gr, vw)
    sg_scr[...] = (g_ret * _sigmoid(g_ret)).astype(BF16)

    gate_a = _sigmoid(proj(o_ga, d_model))
    gate_b = _sigmoid(proj(o_gb, d_model))

    lam_a = lama_ref[...]
    lam_b = lamb_ref[...]
    v = sv[...]
    w = sw[...]
    for n in range(rows):
        rsel = pl.ds(n, SSM_GROUPS, stride=pitch)
        xin_scr[rsel, :] = v
        v, w = (lam_a * v + lam_b * w + s_scr[rsel, :],
                lam_a * w - lam_b * v + ssw_scr[rsel, :])
    sv[...] = v
    sw[...] = w

    for g in range(SSM_GROUPS):
        x_in = xin_scr[g * pitch:g * pitch + rows, :].astype(BF16)
        yc_scr[g] = _dot(uc_scr[g], m_ref[g]) + _dot(x_in, cout_ref[g])

    for c in range(n_chunks):
        rs = slice(c * RET_CHUNK, (c + 1) * RET_CHUNK)
        qsl = [slice(h * RET_QK_DIM, (h + 1) * RET_QK_DIM) for h in range(RET_HEADS)]
        vsl = [slice(h * RET_V_DIM, (h + 1) * RET_V_DIM) for h in range(RET_HEADS)]
        qh = [q_scr[rs, qsl[h]] for h in range(RET_HEADS)]
        kf = [k_scr[rs, qsl[h]] for h in range(RET_HEADS)]
        vh = [v_scr[rs, vsl[h]] for h in range(RET_HEADS)]
        scores = [_dot_nt(qh[h], kf[h].astype(BF16)) * dec_ref[h] for h in range(RET_HEADS)]
        st = [rstate[h] for h in range(RET_HEADS)]
        cross = [_dot(qh[h], st[h].astype(BF16)) * xi_ref[h] for h in range(RET_HEADS)]
        for h in range(RET_HEADS):
            kz_t = (kf[h] * zeta_ref[h]).T.astype(BF16)
            rstate[h] = cd_ref[h] * st[h] + _dot(kz_t, vh[h])
        for h in range(RET_HEADS):
            ret = _dot(scores[h].astype(BF16), vh[h]) + cross[h]
            dev = ret - jnp.mean(ret, axis=-1, keepdims=True)
            rn = dev * lax.rsqrt(jnp.mean(dev * dev, axis=-1, keepdims=True) + EPS)
            gated_scr[rs, vsl[h]] = (rn * sg_scr[rs, vsl[h]].astype(F32)).astype(BF16)

    for qt in range(lane_tiles):
        for half in range(halves):
            pieces = [yc_scr[qt * GROUPS_PER_LANE_TILE + gl, :, half * LANES:(half + 1) * LANES]
                      for gl in range(GROUPS_PER_LANE_TILE)]
            t = _transpose8(pieces, lane_block)
            for i in range(8):
                ynat_scr[qt, pl.ds(8 * half + i, rows, stride=SSM_CHUNK), :] = t[i]
    z = jnp.concatenate([jax.nn.gelu(ynat_scr[qt]).astype(BF16) for qt in range(lane_tiles)], axis=1)

    glu = _dot(z, wglu_ref[...])
    y_b = glu[:, :d_model] * _sigmoid(glu[:, d_model:])
    y_a = _dot(gated_scr[...], wret_ref[...])

    mix = (gate_a * y_a + gate_b * y_b).astype(BF16)
    for rs in row_halves:
        mixed = _dot(mix[rs, :], wout_ref[...])
        o_ref[0, tile_rows(rs), :] = x_ref[0, tile_rows(rs), :] + mixed * _rms_scale(mixed) * gpost_ref[...]


def _ffn_kernel(h_ref, gpre_ref, wg_ref, wu_ref, wd_ref, gpost_ref, o_ref, *, hidden_chunks, row_block):
    for r0 in range(0, h_ref.shape[0], row_block):
        rs = slice(r0, r0 + row_block)
        h = h_ref[rs, :]
        f_in = (h * _rms_scale(h) * gpre_ref[...]).astype(BF16)
        f = None
        for lo, hi in hidden_chunks:
            gate = _dot(f_in, wg_ref[:, lo:hi])
            up = _dot(f_in, wu_ref[:, lo:hi])
            part = _dot((gate * _sigmoid(gate) * up).astype(BF16), wd_ref[lo:hi, :])
            f = part if f is None else f + part
        o_ref[rs, :] = h_ref[rs, :] + f * _rms_scale(f) * gpost_ref[...]


def _resident(shape):
    nd = len(shape)
    return pl.BlockSpec(shape, lambda *_: (0,) * nd, pipeline_mode=pl.Buffered(1))


def _retention_constants():
    c = RET_CHUNK
    log_g = jnp.log(1.0 - 2.0 ** (-5.0 - jnp.arange(RET_HEADS, dtype=F32)))
    idx = jnp.arange(c, dtype=F32)
    diff = idx[:, None] - idx[None, :]
    inner_decay = jnp.where(diff >= 0, jnp.exp(log_g[:, None, None] * jnp.maximum(diff, 0.0)), 0.0)
    zeta = jnp.exp(log_g[:, None] * (c - 1.0 - idx)[None, :])
    xi = jnp.exp(log_g[:, None] * (idx + 1.0)[None, :])
    chunk_decay = jnp.exp(log_g * c)
    zeta_full = jnp.broadcast_to(zeta[:, :, None], (RET_HEADS, c, RET_QK_DIM))
    xi_full = jnp.broadcast_to(xi[:, :, None], (RET_HEADS, c, RET_V_DIM))
    cd_full = jnp.broadcast_to(chunk_decay[:, None, None], (RET_HEADS, 1, RET_V_DIM))
    return inner_decay.astype(F32), xi_full.astype(F32), zeta_full.astype(F32), cd_full.astype(F32)


def _dot_f32(a, b):
    return jnp.dot(a, b, precision=lax.Precision.HIGHEST, preferred_element_type=F32)


def _s5_prep_kernel(*refs):
    for gi in range(refs[0].shape[0]):
        _s5_prep_group(*[r.at[gi] for r in refs])


def _s5_prep_group(arc_ref, aic_ref, arr_ref, air_ref, ldt_ref, btr_ref, bti_ref, ctr_ref, cti_ref,
                   dsk_ref, m_ref, wst_ref, cout_ref, lama_ref, lamb_ref):
    tc = SSM_CHUNK
    dt = jnp.exp(ldt_ref[...])

    def discretise(a_r, a_i):
        da_r = dt * a_r
        da_i = dt * a_i
        mag = jnp.exp(da_r)
        l_r = mag * jnp.cos(da_i)
        l_i = mag * jnp.sin(da_i)
        den = a_r * a_r + a_i * a_i
        num_r = l_r - 1.0
        return l_r, l_i, (num_r * a_r + l_i * a_i) / den, (l_i * a_r - num_r * a_i) / den

    lc_r, lc_i, _, _ = discretise(arc_ref[...], aic_ref[...])
    lr_r, lr_i, f_r, f_i = discretise(arr_ref[...], air_ref[...])
    bbt_r = f_r * btr_ref[...] - f_i * bti_ref[...]
    bbt_i = f_r * bti_ref[...] + f_i * btr_ref[...]

    lane_d = lax.broadcasted_iota(jnp.int32, (SSM_STATE, SSM_CHUNK_W), 1) // SSM_GROUP_CH
    pc_r = jnp.ones((SSM_STATE, 1), F32)
    pc_i = jnp.zeros((SSM_STATE, 1), F32)
    pw_r = jnp.zeros((SSM_STATE, SSM_CHUNK_W), F32)
    pw_i = jnp.zeros((SSM_STATE, SSM_CHUNK_W), F32)
    for d in range(tc):
        sel = lane_d == d
        pw_r = jnp.where(sel, pc_r, pw_r)
        pw_i = jnp.where(sel, pc_i, pw_i)
        pc_r, pc_i = pc_r * lc_r - pc_i * lc_i, pc_r * lc_i + pc_i * lc_r
    ct_r = ctr_ref[...]
    ct_i = cti_ref[...]

    r0 = (_dot_f32(bbt_r, pw_r * ct_r - pw_i * ct_i)
          - _dot_f32(bbt_i, pw_i * ct_r + pw_r * ct_i))
    row = lax.broadcasted_iota(jnp.int32, (SSM_GROUP_CH, SSM_CHUNK_W), 0)
    lane = lax.broadcasted_iota(jnp.int32, (SSM_GROUP_CH, SSM_CHUNK_W), 1)
    r0 = r0 + jnp.where(row == lane, dsk_ref[...], 0.0)
    for j in range(tc):
        blk = r0 if j == 0 else jnp.where(lane >= SSM_GROUP_CH * j, pltpu.roll(r0, SSM_GROUP_CH * j, 1), 0.0)
        m_ref[j * SSM_GROUP_CH:(j + 1) * SSM_GROUP_CH, :] = blk.astype(BF16)

    p1_r = pw_r * lc_r - pw_i * lc_i
    p1_i = pw_r * lc_i + pw_i * lc_r
    cout_ref[0:SSM_STATE, :] = (p1_r * ct_r - p1_i * ct_i).astype(BF16)
    cout_ref[SSM_STATE:2 * SSM_STATE, :] = (-(p1_i * ct_r + p1_r * ct_i)).astype(BF16)

    pr_r = jnp.ones((1, SSM_STATE), F32)
    pr_i = jnp.zeros((1, SSM_STATE), F32)
    powers = []
    for d in range(tc + 1):
        powers.append((pr_r, pr_i))
        pr_r, pr_i = pr_r * lr_r - pr_i * lr_i, pr_r * lr_i + pr_i * lr_r
    for j in range(tc):
        q_r, q_i = powers[tc - 1 - j]
        blk = jnp.concatenate([bbt_r * q_r - bbt_i * q_i, bbt_i * q_r + bbt_r * q_i], axis=1)
        wst_ref[j * SSM_GROUP_CH:(j + 1) * SSM_GROUP_CH, :] = blk.astype(BF16)
    q_r, q_i = powers[tc]
    lama_ref[...] = jnp.concatenate([q_r, q_r], axis=1)
    lamb_ref[...] = jnp.concatenate([-q_i, q_i], axis=1)


def _s5_prep_call(a_re, a_im, log_dt, b_re, b_im, c_re, c_im, d_skip):
    g, p, c = SSM_GROUPS, SSM_STATE, SSM_GROUP_CH
    tile_c = lambda m: jnp.tile(jnp.swapaxes(m, 1, 2), (1, 1, SSM_CHUNK))
    operands = [a_re.reshape(g, p, 1), a_im.reshape(g, p, 1), a_re.reshape(g, 1, p), a_im.reshape(g, 1, p),
                log_dt.reshape(g, 1, 1), jnp.swapaxes(b_re, 1, 2), jnp.swapaxes(b_im, 1, 2),
                tile_c(c_re), tile_c(c_im), jnp.tile(d_skip.reshape(g, 1, c), (1, 1, SSM_CHUNK))]
    out_shapes = [((SSM_CHUNK_W, SSM_CHUNK_W), BF16), ((SSM_CHUNK_W, 2 * p), BF16), ((2 * p, SSM_CHUNK_W), BF16),
                  ((1, 2 * p), F32), ((1, 2 * p), F32)]
    spec = lambda shp: pl.BlockSpec((PREP_GROUPS_PER_STEP,) + tuple(shp), lambda i: (i, 0, 0))
    m, wst, cout, lam_a, lam_b = pl.pallas_call(
        _s5_prep_kernel,
        grid=(g // PREP_GROUPS_PER_STEP,),
        in_specs=[spec(o.shape[1:]) for o in operands],
        out_specs=[spec(shp) for shp, _ in out_shapes],
        out_shape=[jax.ShapeDtypeStruct((g,) + shp, dt) for shp, dt in out_shapes],
        compiler_params=pltpu.CompilerParams(dimension_semantics=("arbitrary",)),
        name="s5_prep",
    )(*operands)
    return m, wst, cout, lam_a.reshape(g, 2 * p), lam_b.reshape(g, 2 * p)


def _mixer_call(x, pos_f, invf, gpre, win, ret_consts, wret, s5_ops, wglu, wout, gpost):
    bsz, seq, d_model = x.shape
    tile = min(MIXER_TILE, seq)
    tm = min(MIXER_SUB_TILE, tile)
    assert seq % tile == 0 and tile % tm == 0 and tm % RET_CHUNK == 0
    rows = tm // SSM_CHUNK
    pitch = _state_row_pitch(rows)
    inner_decay, xi_full, zeta_full, cd_full = ret_consts
    m, wst, cout, lam_a, lam_b = s5_ops
    qkw = RET_HEADS * RET_QK_DIM
    vw = RET_HEADS * RET_V_DIM
    ssm_w = SSM_GROUPS * SSM_GROUP_CH
    consts = [invf, gpre, win, inner_decay, xi_full, zeta_full, cd_full, wret, m, wst, cout,
              lam_a, lam_b, wglu, wout, gpost]
    in_specs = [pl.BlockSpec((1, tile, d_model), lambda b, t: (b, t, 0)),
                pl.BlockSpec((1, seq // tile, tile // LANES, LANES), lambda b, t: (b, 0, 0, 0))]
    in_specs += [_resident(c.shape) for c in consts]
    scratch = [
        pltpu.VMEM((RET_HEADS, RET_QK_DIM, RET_V_DIM), F32),
        pltpu.VMEM((SSM_GROUPS, 2 * SSM_STATE), F32),
        pltpu.VMEM((SSM_GROUPS, 2 * SSM_STATE), F32),
        pltpu.VMEM((tm, d_model), BF16),
        pltpu.VMEM((tm, qkw), BF16),
        pltpu.VMEM((tm, qkw), F32),
        pltpu.VMEM((tm, vw), BF16),
        pltpu.VMEM((tm, vw), BF16),
        pltpu.VMEM((tm, vw), BF16),
        pltpu.VMEM((ssm_w // LANES, tm, LANES), F32),
        pltpu.VMEM((SSM_GROUPS, rows, SSM_CHUNK_W), BF16),
        pltpu.VMEM((SSM_GROUPS * pitch, 2 * SSM_STATE), F32),
        pltpu.VMEM((SSM_GROUPS * pitch, 2 * SSM_STATE), F32),
        pltpu.VMEM((SSM_GROUPS * pitch, 2 * SSM_STATE), F32),
        pltpu.VMEM((SSM_GROUPS, rows, SSM_CHUNK_W), F32),
        pltpu.VMEM((ssm_w // LANES, tm, LANES), F32),
    ]
    return pl.pallas_call(
        functools.partial(_mixer_kernel, tm=tm),
        grid=(bsz, seq // tile),
        in_specs=in_specs,
        out_specs=pl.BlockSpec((1, tile, d_model), lambda b, t: (b, t, 0)),
        out_shape=jax.ShapeDtypeStruct(x.shape, F32),
        scratch_shapes=scratch,
        compiler_params=pltpu.CompilerParams(
            dimension_semantics=("arbitrary", "arbitrary"),
            vmem_limit_bytes=VMEM_LIMIT_BYTES),
        name="mixer",
    )(x, pos_f.reshape(bsz, seq // tile, tile // LANES, LANES), *consts)


def _ffn_call(h, gpre, wg, wu, wd, gpost):
    n_tok, d_model = h.shape
    hidden = wg.shape[1]
    tm = min(FFN_TILE, n_tok)
    assert n_tok % tm == 0
    step = 1024
    hidden_chunks = tuple((lo, min(lo + step, hidden)) for lo in range(0, hidden, step))
    consts = [gpre, wg, wu, wd, gpost]
    return pl.pallas_call(
        functools.partial(_ffn_kernel, hidden_chunks=hidden_chunks, row_block=min(FFN_ROW_BLOCK, tm)),
        grid=(n_tok // tm,),
        in_specs=[pl.BlockSpec((tm, d_model), lambda i: (i, 0))] + [_resident(c.shape) for c in consts],
        out_specs=pl.BlockSpec((tm, d_model), lambda i: (i, 0)),
        out_shape=jax.ShapeDtypeStruct(h.shape, F32),
        compiler_params=pltpu.CompilerParams(
            dimension_semantics=("arbitrary",),
            vmem_limit_bytes=VMEM_LIMIT_BYTES),
        name="ffn",
    )(h, *consts)


def kernel(x, positions, mix_pre_norm, w_in, ssm_a_re, ssm_a_im, ssm_log_dt, ssm_b_re, ssm_b_im, ssm_c_re, ssm_c_im, ssm_d, w_glu_val, w_glu_gate, w_ret_up, w_out, mix_post_norm, ffn_pre_norm, w_ffn_gate, w_ffn_up, w_ffn_down, ffn_post_norm):
    bsz, seq, d_model = x.shape
    depth = w_in.shape[0]
    pos_f = positions.astype(F32)
    inv_freq = ROPE_BASE ** (-jnp.arange(0, RET_QK_DIM, 2, dtype=F32) / RET_QK_DIM)
    invf = jnp.concatenate([inv_freq, inv_freq]).reshape(RET_QK_DIM, 1)
    ret_consts = _retention_constants()
    h = x
    for layer in range(depth):
        s5_ops = _s5_prep_call(ssm_a_re[layer], ssm_a_im[layer], ssm_log_dt[layer], ssm_b_re[layer],
                               ssm_b_im[layer], ssm_c_re[layer], ssm_c_im[layer], ssm_d[layer])
        wglu = jnp.concatenate([w_glu_val[layer], w_glu_gate[layer]], axis=1).astype(BF16)
        h = _mixer_call(h, pos_f, invf, mix_pre_norm[layer].reshape(1, d_model), w_in[layer].astype(BF16),
                        ret_consts, w_ret_up[layer].astype(BF16), s5_ops, wglu, w_out[layer].astype(BF16),
                        mix_post_norm[layer].reshape(1, d_model))
        h = _ffn_call(h.reshape(bsz * seq, d_model), ffn_pre_norm[layer].reshape(1, d_model),
                      w_ffn_gate[layer].astype(BF16), w_ffn_up[layer].astype(BF16),
                      w_ffn_down[layer].astype(BF16), ffn_post_norm[layer].reshape(1, d_model)
                      ).reshape(bsz, seq, d_model)
    return h
```

```python
import functools

import jax
import jax.numpy as jnp
from jax import lax
from jax.experimental import pallas as pl
from jax.experimental.pallas import tpu as pltpu

F32 = jnp.float32
BF16 = jnp.bfloat16

EPS = 1e-6
ROPE_BASE = 10000.0
RET_HEADS = 4
RET_QK_DIM = 128
RET_V_DIM = 256
RET_CHUNK = 128
SSM_GROUP_CH = 16
SSM_GROUPS = 32
SSM_STATE = 64
SSM_CHUNK = 8
SSM_CHUNK_W = SSM_CHUNK * SSM_GROUP_CH
LANES = 128
GROUPS_PER_LANE_TILE = LANES // SSM_GROUP_CH
VMEM_LIMIT_BYTES = 58 * 1024 * 1024

MIXER_TILE = 1024
MIXER_SUB_TILE = 512
FFN_TILE = 1024
FFN_ROW_BLOCK = 256
PREP_GROUPS_PER_STEP = 4


def _dot(a, b):
    return jnp.dot(a, b, preferred_element_type=F32)


def _dot_nt(a, b):
    return lax.dot_general(a, b, (((1,), (1,)), ((), ())), preferred_element_type=F32)


def _sigmoid(x):
    return 0.5 * jnp.tanh(0.5 * x) + 0.5


def _rms_scale(x):
    return lax.rsqrt(jnp.mean(x * x, axis=-1, keepdims=True) + EPS)


def _transpose8(pieces, lane_block):
    a = list(pieces)
    s = 4
    while s >= 1:
        keep = (lane_block & s) == 0
        shift = SSM_GROUP_CH * s
        for i in range(8):
            if i & s:
                continue
            lo, hi = a[i], a[i + s]
            a[i] = jnp.where(keep, lo, pltpu.roll(hi, shift, 1))
            a[i + s] = jnp.where(keep, pltpu.roll(lo, LANES - shift, 1), hi)
        s //= 2
    return a


def _state_row_pitch(rows):
    assert rows % 8 == 0
    return rows + 4


def _mixer_kernel(*refs, tm):
    x_ref = refs[0]
    rstate, sv, sw = refs[19:22]

    @pl.when(pl.program_id(1) == 0)
    def _():
        rstate[...] = jnp.zeros_like(rstate)
        sv[...] = jnp.zeros_like(sv)
        sw[...] = jnp.zeros_like(sw)

    for base in range(0, x_ref.shape[1], tm):
        _mixer_sub_tile(base, tm, *refs)


def _mixer_sub_tile(base, tm, x_ref, pos_ref, invf_ref, gpre_ref, win_ref, dec_ref, xi_ref, zeta_ref, cd_ref,
                    wret_ref, m_ref, wst_ref, cout_ref, lama_ref, lamb_ref, wglu_ref, wout_ref,
                    gpost_ref, o_ref,
                    rstate, sv, sw, u_scr, q_scr, k_scr, v_scr, sg_scr, gated_scr, ussm_scr, uc_scr,
                    s_scr, ssw_scr, xin_scr, yc_scr, ynat_scr):
    n_chunks = tm // RET_CHUNK
    rows = tm // SSM_CHUNK
    pitch = _state_row_pitch(rows)
    qkw = RET_HEADS * RET_QK_DIM
    vw = RET_HEADS * RET_V_DIM
    ssm_w = SSM_GROUPS * SSM_GROUP_CH
    lane_tiles = ssm_w // LANES
    halves = SSM_CHUNK // 8
    d_model = x_ref.shape[2]
    hq = RET_QK_DIM // 2
    o_v = 2 * qkw
    o_gr = o_v + vw
    o_ssm = o_gr + vw
    o_ga = o_ssm + ssm_w
    o_gb = o_ga + d_model

    def tile_rows(rs):
        return slice(base + rs.start, base + rs.stop)

    def proj(lo, width):
        return _dot(u_scr[...], win_ref[:, lo:lo + width])

    lane_block = lax.broadcasted_iota(jnp.int32, (rows, LANES), 1) // SSM_GROUP_CH


    row_halves = [slice(0, tm // 2), slice(tm // 2, tm)]
    for rs in row_halves:
        x = x_ref[0, tile_rows(rs), :]
        u_scr[rs, :] = (x * _rms_scale(x) * gpre_ref[...]).astype(BF16)

    for rs in row_halves:
        u_ssm = _dot(u_scr[rs, :], win_ref[:, o_ssm:o_ssm + ssm_w])
        for qt in range(lane_tiles):
            ussm_scr[qt, rs, :] = u_ssm[:, qt * LANES:(qt + 1) * LANES]

    q = proj(0, qkw)
    k = proj(qkw, qkw)

    for qt in range(lane_tiles):
        for half in range(halves):
            pieces = [ussm_scr[qt, pl.ds(8 * half + j, rows, stride=SSM_CHUNK), :] for j in range(8)]
            t = _transpose8(pieces, lane_block)
            for gl in range(GROUPS_PER_LANE_TILE):
                uc_scr[qt * GROUPS_PER_LANE_TILE + gl, :, half * LANES:(half + 1) * LANES] = t[gl].astype(BF16)

    half_tm = tm // 2
    pos_rows = pos_ref[0, pl.program_id(1), base // LANES:(base + tm) // LANES, :]
    half_rows = half_tm // LANES
    pos_lo = jnp.concatenate([pos_rows[r:r + 1, :] for r in range(half_rows)], axis=1)
    pos_hi = jnp.concatenate([pos_rows[r:r + 1, :] for r in range(half_rows, 2 * half_rows)], axis=1)
    upper = lax.broadcasted_iota(jnp.int32, (LANES, half_tm), 0) < hq
    ang_t = invf_ref[...] * jnp.where(upper, pos_lo, pos_hi)
    cs = jnp.cos(ang_t).T
    sn = jnp.sin(ang_t).T
    first = lax.broadcasted_iota(jnp.int32, (half_tm, LANES), 1) < hq
    cs_sw = pltpu.roll(cs, hq, 1)
    sn_sw = pltpu.roll(sn, hq, 1)
    cos2 = jnp.concatenate([jnp.where(first, cs, cs_sw), jnp.where(first, cs_sw, cs)], axis=0)
    sin2 = jnp.concatenate([jnp.where(first, -sn, sn_sw), jnp.where(first, -sn_sw, sn)], axis=0)
    kscale = RET_QK_DIM ** -0.5
    for h in range(RET_HEADS):
        sl = slice(h * RET_QK_DIM, (h + 1) * RET_QK_DIM)
        qh = q[:, sl]
        kh = k[:, sl]
        q_scr[:, sl] = (qh * cos2 + pltpu.roll(qh, hq, 1) * sin2).astype(BF16)
        k_scr[:, sl] = (kh * cos2 + pltpu.roll(kh, hq, 1) * sin2) * kscale

    for g in range(SSM_GROUPS):
        s_g = _dot(uc_scr[g], wst_ref[g])
        s_scr[g * pitch:g * pitch + rows, :] = s_g
        ssw_scr[g * pitch:g * pitch + rows, :] = pltpu.roll(s_g, SSM_STATE, 1)

    v_scr[...] = proj(o_v, vw).astype(BF16)
    g_ret = proj(o_gr, vw)
    sg_scr[...] = (g_ret * _sigmoid(g_ret)).astype(BF16)

    gate_a = _sigmoid(proj(o_ga, d_model))
    gate_b = _sigmoid(proj(o_gb, d_model))

    lam_a = lama_ref[...]
    lam_b = lamb_ref[...]
    v = sv[...]
    w = sw[...]
    for n in range(rows):
        rsel = pl.ds(n, SSM_GROUPS, stride=pitch)
        xin_scr[rsel, :] = v
        v, w = (lam_a * v + lam_b * w + s_scr[rsel, :],
                lam_a * w - lam_b * v + ssw_scr[rsel, :])
    sv[...] = v
    sw[...] = w

    for g in range(SSM_GROUPS):
        x_in = xin_scr[g * pitch:g * pitch + rows, :].astype(BF16)
        yc_scr[g] = _dot(uc_scr[g], m_ref[g]) + _dot(x_in, cout_ref[g])

    for c in range(n_chunks):
        rs = slice(c * RET_CHUNK, (c + 1) * RET_CHUNK)
        qsl = [slice(h * RET_QK_DIM, (h + 1) * RET_QK_DIM) for h in range(RET_HEADS)]
        vsl = [slice(h * RET_V_DIM, (h + 1) * RET_V_DIM) for h in range(RET_HEADS)]
        qh = [q_scr[rs, qsl[h]] for h in range(RET_HEADS)]
        kf = [k_scr[rs, qsl[h]] for h in range(RET_HEADS)]
        vh = [v_scr[rs, vsl[h]] for h in range(RET_HEADS)]
        scores = [_dot_nt(qh[h], kf[h].astype(BF16)) * dec_ref[h] for h in range(RET_HEADS)]
        st = [rstate[h] for h in range(RET_HEADS)]
        cross = [_dot(qh[h], st[h].astype(BF16)) * xi_ref[h] for h in range(RET_HEADS)]
        for h in range(RET_HEADS):
            kz_t = (kf[h] * zeta_ref[h]).T.astype(BF16)
            rstate[h] = cd_ref[h] * st[h] + _dot(kz_t, vh[h])
        for h in range(RET_HEADS):
            ret = _dot(scores[h].astype(BF16), vh[h]) + cross[h]
            dev = ret - jnp.mean(ret, axis=-1, keepdims=True)
            rn = dev * lax.rsqrt(jnp.mean(dev * dev, axis=-1, keepdims=True) + EPS)
            gated_scr[rs, vsl[h]] = (rn * sg_scr[rs, vsl[h]].astype(F32)).astype(BF16)

    for qt in range(lane_tiles):
        for half in range(halves):
            pieces = [yc_scr[qt * GROUPS_PER_LANE_TILE + gl, :, half * LANES:(half + 1) * LANES]
                      for gl in range(GROUPS_PER_LANE_TILE)]
            t = _transpose8(pieces, lane_block)
            for i in range(8):
                ynat_scr[qt, pl.ds(8 * half + i, rows, stride=SSM_CHUNK), :] = t[i]
    z = jnp.concatenate([jax.nn.gelu(ynat_scr[qt]).astype(BF16) for qt in range(lane_tiles)], axis=1)

    glu = _dot(z, wglu_ref[...])
    y_b = glu[:, :d_model] * _sigmoid(glu[:, d_model:])
    y_a = _dot(gated_scr[...], wret_ref[...])

    mix = (gate_a * y_a + gate_b * y_b).astype(BF16)
    for rs in row_halves:
        mixed = _dot(mix[rs, :], wout_ref[...])
        o_ref[0, tile_rows(rs), :] = x_ref[0, tile_rows(rs), :] + mixed * _rms_scale(mixed) * gpost_ref[...]


def _ffn_kernel(h_ref, gpre_ref, wg_ref, wu_ref, wd_ref, gpost_ref, o_ref, *, hidden_chunks, row_block):
    for r0 in range(0, h_ref.shape[0], row_block):
        rs = slice(r0, r0 + row_block)
        h = h_ref[rs, :]
        f_in = (h * _rms_scale(h) * gpre_ref[...]).astype(BF16)
        f = None
        for lo, hi in hidden_chunks:
            gate = _dot(f_in, wg_ref[:, lo:hi])
            up = _dot(f_in, wu_ref[:, lo:hi])
            part = _dot((gate * _sigmoid(gate) * up).astype(BF16), wd_ref[lo:hi, :])
            f = part if f is None else f + part
        o_ref[rs, :] = h_ref[rs, :] + f * _rms_scale(f) * gpost_ref[...]


def _resident(shape):
    nd = len(shape)
    return pl.BlockSpec(shape, lambda *_: (0,) * nd, pipeline_mode=pl.Buffered(1))


def _retention_constants():
    c = RET_CHUNK
    log_g = jnp.log(1.0 - 2.0 ** (-5.0 - jnp.arange(RET_HEADS, dtype=F32)))
    idx = jnp.arange(c, dtype=F32)
    diff = idx[:, None] - idx[None, :]
    inner_decay = jnp.where(diff >= 0, jnp.exp(log_g[:, None, None] * jnp.maximum(diff, 0.0)), 0.0)
    zeta = jnp.exp(log_g[:, None] * (c - 1.0 - idx)[None, :])
    xi = jnp.exp(log_g[:, None] * (idx + 1.0)[None, :])
    chunk_decay = jnp.exp(log_g * c)
    zeta_full = jnp.broadcast_to(zeta[:, :, None], (RET_HEADS, c, RET_QK_DIM))
    xi_full = jnp.broadcast_to(xi[:, :, None], (RET_HEADS, c, RET_V_DIM))
    cd_full = jnp.broadcast_to(chunk_decay[:, None, None], (RET_HEADS, 1, RET_V_DIM))
    return inner_decay.astype(F32), xi_full.astype(F32), zeta_full.astype(F32), cd_full.astype(F32)


def _dot_f32(a, b):
    return jnp.dot(a, b, precision=lax.Precision.HIGHEST, preferred_element_type=F32)


def _s5_prep_kernel(*refs):
    for gi in range(refs[0].shape[0]):
        _s5_prep_group(*[r.at[gi] for r in refs])


def _s5_prep_group(arc_ref, aic_ref, arr_ref, air_ref, ldt_ref, btr_ref, bti_ref, ctr_ref, cti_ref,
                   dsk_ref, m_ref, wst_ref, cout_ref, lama_ref, lamb_ref):
    tc = SSM_CHUNK
    dt = jnp.exp(ldt_ref[...])

    def discretise(a_r, a_i):
        da_r = dt * a_r
        da_i = dt * a_i
        mag = jnp.exp(da_r)
        l_r = mag * jnp.cos(da_i)
        l_i = mag * jnp.sin(da_i)
        den = a_r * a_r + a_i * a_i
        num_r = l_r - 1.0
        return l_r, l_i, (num_r * a_r + l_i * a_i) / den, (l_i * a_r - num_r * a_i) / den

    lc_r, lc_i, _, _ = discretise(arc_ref[...], aic_ref[...])
    lr_r, lr_i, f_r, f_i = discretise(arr_ref[...], air_ref[...])
    bbt_r = f_r * btr_ref[...] - f_i * bti_ref[...]
    bbt_i = f_r * bti_ref[...] + f_i * btr_ref[...]

    lane_d = lax.broadcasted_iota(jnp.int32, (SSM_STATE, SSM_CHUNK_W), 1) // SSM_GROUP_CH
    pc_r = jnp.ones((SSM_STATE, 1), F32)
    pc_i = jnp.zeros((SSM_STATE, 1), F32)
    pw_r = jnp.zeros((SSM_STATE, SSM_CHUNK_W), F32)
    pw_i = jnp.zeros((SSM_STATE, SSM_CHUNK_W), F32)
    for d in range(tc):
        sel = lane_d == d
        pw_r = jnp.where(sel, pc_r, pw_r)
        pw_i = jnp.where(sel, pc_i, pw_i)
        pc_r, pc_i = pc_r * lc_r - pc_i * lc_i, pc_r * lc_i + pc_i * lc_r
    ct_r = ctr_ref[...]
    ct_i = cti_ref[...]

    r0 = (_dot_f32(bbt_r, pw_r * ct_r - pw_i * ct_i)
          - _dot_f32(bbt_i, pw_i * ct_r + pw_r * ct_i))
    row = lax.broadcasted_iota(jnp.int32, (SSM_GROUP_CH, SSM_CHUNK_W), 0)
    lane = lax.broadcasted_iota(jnp.int32, (SSM_GROUP_CH, SSM_CHUNK_W), 1)
    r0 = r0 + jnp.where(row == lane, dsk_ref[...], 0.0)
    for j in range(tc):
        blk = r0 if j == 0 else jnp.where(lane >= SSM_GROUP_CH * j, pltpu.roll(r0, SSM_GROUP_CH * j, 1), 0.0)
        m_ref[j * SSM_GROUP_CH:(j + 1) * SSM_GROUP_CH, :] = blk.astype(BF16)

    p1_r = pw_r * lc_r - pw_i * lc_i
    p1_i = pw_r * lc_i + pw_i * lc_r
    cout_ref[0:SSM_STATE, :] = (p1_r * ct_r - p1_i * ct_i).astype(BF16)
    cout_ref[SSM_STATE:2 * SSM_STATE, :] = (-(p1_i * ct_r + p1_r * ct_i)).astype(BF16)

    pr_r = jnp.ones((1, SSM_STATE), F32)
    pr_i = jnp.zeros((1, SSM_STATE), F32)
    powers = []
    for d in range(tc + 1):
        powers.append((pr_r, pr_i))
        pr_r, pr_i = pr_r * lr_r - pr_i * lr_i, pr_r * lr_i + pr_i * lr_r
    for j in range(tc):
        q_r, q_i = powers[tc - 1 - j]
        blk = jnp.concatenate([bbt_r * q_r - bbt_i * q_i, bbt_i * q_r + bbt_r * q_i], axis=1)
        wst_ref[j * SSM_GROUP_CH:(j + 1) * SSM_GROUP_CH, :] = blk.astype(BF16)
    q_r, q_i = powers[tc]
    lama_ref[...] = jnp.concatenate([q_r, q_r], axis=1)
    lamb_ref[...] = jnp.concatenate([-q_i, q_i], axis=1)


def _s5_prep_call(a_re, a_im, log_dt, b_re, b_im, c_re, c_im, d_skip):
    g, p, c = SSM_GROUPS, SSM_STATE, SSM_GROUP_CH
    tile_c = lambda m: jnp.tile(jnp.swapaxes(m, 1, 2), (1, 1, SSM_CHUNK))
    operands = [a_re.reshape(g, p, 1), a_im.reshape(g, p, 1), a_re.reshape(g, 1, p), a_im.reshape(g, 1, p),
                log_dt.reshape(g, 1, 1), jnp.swapaxes(b_re, 1, 2), jnp.swapaxes(b_im, 1, 2),
                tile_c(c_re), tile_c(c_im), jnp.tile(d_skip.reshape(g, 1, c), (1, 1, SSM_CHUNK))]
    out_shapes = [((SSM_CHUNK_W, SSM_CHUNK_W), BF16), ((SSM_CHUNK_W, 2 * p), BF16), ((2 * p, SSM_CHUNK_W), BF16),
                  ((1, 2 * p), F32), ((1, 2 * p), F32)]
    spec = lambda shp: pl.BlockSpec((PREP_GROUPS_PER_STEP,) + tuple(shp), lambda i: (i, 0, 0))
    m, wst, cout, lam_a, lam_b = pl.pallas_call(
        _s5_prep_kernel,
        grid=(g // PREP_GROUPS_PER_STEP,),
        in_specs=[spec(o.shape[1:]) for o in operands],
        out_specs=[spec(shp) for shp, _ in out_shapes],
        out_shape=[jax.ShapeDtypeStruct((g,) + shp, dt) for shp, dt in out_shapes],
        compiler_params=pltpu.CompilerParams(dimension_semantics=("arbitrary",)),
        name="s5_prep",
    )(*operands)
    return m, wst, cout, lam_a.reshape(g, 2 * p), lam_b.reshape(g, 2 * p)


def _mixer_call(x, pos_f, invf, gpre, win, ret_consts, wret, s5_ops, wglu, wout, gpost):
    bsz, seq, d_model = x.shape
    tile = min(MIXER_TILE, seq)
    tm = min(MIXER_SUB_TILE, tile)
    assert seq % tile == 0 and tile % tm == 0 and tm % RET_CHUNK == 0
    rows = tm // SSM_CHUNK
    pitch = _state_row_pitch(rows)
    inner_decay, xi_full, zeta_full, cd_full = ret_consts
    m, wst, cout, lam_a, lam_b = s5_ops
    qkw = RET_HEADS * RET_QK_DIM
    vw = RET_HEADS * RET_V_DIM
    ssm_w = SSM_GROUPS * SSM_GROUP_CH
    consts = [invf, gpre, win, inner_decay, xi_full, zeta_full, cd_full, wret, m, wst, cout,
              lam_a, lam_b, wglu, wout, gpost]
    in_specs = [pl.BlockSpec((1, tile, d_model), lambda b, t: (b, t, 0)),
                pl.BlockSpec((1, seq // tile, tile // LANES, LANES), lambda b, t: (b, 0, 0, 0))]
    in_specs += [_resident(c.shape) for c in consts]
    scratch = [
        pltpu.VMEM((RET_HEADS, RET_QK_DIM, RET_V_DIM), F32),
        pltpu.VMEM((SSM_GROUPS, 2 * SSM_STATE), F32),
        pltpu.VMEM((SSM_GROUPS, 2 * SSM_STATE), F32),
        pltpu.VMEM((tm, d_model), BF16),
        pltpu.VMEM((tm, qkw), BF16),
        pltpu.VMEM((tm, qkw), F32),
        pltpu.VMEM((tm, vw), BF16),
        pltpu.VMEM((tm, vw), BF16),
        pltpu.VMEM((tm, vw), BF16),
        pltpu.VMEM((ssm_w // LANES, tm, LANES), F32),
        pltpu.VMEM((SSM_GROUPS, rows, SSM_CHUNK_W), BF16),
        pltpu.VMEM((SSM_GROUPS * pitch, 2 * SSM_STATE), F32),
        pltpu.VMEM((SSM_GROUPS * pitch, 2 * SSM_STATE), F32),
        pltpu.VMEM((SSM_GROUPS * pitch, 2 * SSM_STATE), F32),
        pltpu.VMEM((SSM_GROUPS, rows, SSM_CHUNK_W), F32),
        pltpu.VMEM((ssm_w // LANES, tm, LANES), F32),
    ]
    return pl.pallas_call(
        functools.partial(_mixer_kernel, tm=tm),
        grid=(bsz, seq // tile),
        in_specs=in_specs,
        out_specs=pl.BlockSpec((1, tile, d_model), lambda b, t: (b, t, 0)),
        out_shape=jax.ShapeDtypeStruct(x.shape, F32),
        scratch_shapes=scratch,
        compiler_params=pltpu.CompilerParams(
            dimension_semantics=("arbitrary", "arbitrary"),
            vmem_limit_bytes=VMEM_LIMIT_BYTES),
        name="mixer",
    )(x, pos_f.reshape(bsz, seq // tile, tile // LANES, LANES), *consts)


def _ffn_call(h, gpre, wg, wu, wd, gpost):
    n_tok, d_model = h.shape
    hidden = wg.shape[1]
    tm = min(FFN_TILE, n_tok)
    assert n_tok % tm == 0
    step = 1024
    hidden_chunks = tuple((lo, min(lo + step, hidden)) for lo in range(0, hidden, step))
    consts = [gpre, wg, wu, wd, gpost]
    return pl.pallas_call(
        functools.partial(_ffn_kernel, hidden_chunks=hidden_chunks, row_block=min(FFN_ROW_BLOCK, tm)),
        grid=(n_tok // tm,),
        in_specs=[pl.BlockSpec((tm, d_model), lambda i: (i, 0))] + [_resident(c.shape) for c in consts],
        out_specs=pl.BlockSpec((tm, d_model), lambda i: (i, 0)),
        out_shape=jax.ShapeDtypeStruct(h.shape, F32),
        compiler_params=pltpu.CompilerParams(
            dimension_semantics=("arbitrary",),
            vmem_limit_bytes=VMEM_LIMIT_BYTES),
        name="ffn",
    )(h, *consts)


def kernel(x, positions, mix_pre_norm, w_in, ssm_a_re, ssm_a_im, ssm_log_dt, ssm_b_re, ssm_b_im, ssm_c_re, ssm_c_im, ssm_d, w_glu_val, w_glu_gate, w_ret_up, w_out, mix_post_norm, ffn_pre_norm, w_ffn_gate, w_ffn_up, w_ffn_down, ffn_post_norm):
    bsz, seq, d_model = x.shape
    depth = w_in.shape[0]
    pos_f = positions.astype(F32)
    inv_freq = ROPE_BASE ** (-jnp.arange(0, RET_QK_DIM, 2, dtype=F32) / RET_QK_DIM)
    invf = jnp.concatenate([inv_freq, inv_freq]).reshape(RET_QK_DIM, 1)
    ret_consts = _retention_constants()
    h = x
    for layer in range(depth):
        s5_ops = _s5_prep_call(ssm_a_re[layer], ssm_a_im[layer], ssm_log_dt[layer], ssm_b_re[layer],
                               ssm_b_im[layer], ssm_c_re[layer], ssm_c_im[layer], ssm_d[layer])
        wglu = jnp.concatenate([w_glu_val[layer], w_glu_gate[layer]], axis=1).astype(BF16)
        h = _mixer_call(h, pos_f, invf, mix_pre_norm[layer].reshape(1, d_model), w_in[layer].astype(BF16),
                        ret_consts, w_ret_up[layer].astype(BF16), s5_ops, wglu, w_out[layer].astype(BF16),
                        mix_post_norm[layer].reshape(1, d_model))
        h = _ffn_call(h.reshape(bsz * seq, d_model), ffn_pre_norm[layer].reshape(1, d_model),
                      w_ffn_gate[layer].astype(BF16), w_ffn_up[layer].astype(BF16),
                      w_ffn_down[layer].astype(BF16), ffn_post_norm[layer].reshape(1, d_model)
                      ).reshape(bsz, seq, d_model)
    return h
```

```python
import functools

import jax
import jax.numpy as jnp
from jax import lax
from jax.experimental import pallas as pl
from jax.experimental.pallas import tpu as pltpu

F32 = jnp.float32
BF16 = jnp.bfloat16

EPS = 1e-6
ROPE_BASE = 10000.0
RET_HEADS = 4
RET_QK_DIM = 128
RET_V_DIM = 256
RET_CHUNK = 128
SSM_GROUP_CH = 16
SSM_GROUPS = 32
SSM_STATE = 64
SSM_CHUNK = 8
SSM_CHUNK_W = SSM_CHUNK * SSM_GROUP_CH
LANES = 128
GROUPS_PER_LANE_TILE = LANES // SSM_GROUP_CH
VMEM_LIMIT_BYTES = 58 * 1024 * 1024

MIXER_TILE = 1024
MIXER_SUB_TILE = 512
FFN_TILE = 1024
FFN_ROW_BLOCK = 256
PREP_GROUPS_PER_STEP = 4


def _dot(a, b):
    return jnp.dot(a, b, preferred_element_type=F32)


def _dot_nt(a, b):
    return lax.dot_general(a, b, (((1,), (1,)), ((), ())), preferred_element_type=F32)


def _sigmoid(x):
    return 0.5 * jnp.tanh(0.5 * x) + 0.5


def _rms_scale(x):
    return lax.rsqrt(jnp.mean(x * x, axis=-1, keepdims=True) + EPS)


def _transpose8(pieces, lane_block):
    a = list(pieces)
    s = 4
    while s >= 1:
        keep = (lane_block & s) == 0
        shift = SSM_GROUP_CH * s
        for i in range(8):
            if i & s:
                continue
            lo, hi = a[i], a[i + s]
            a[i] = jnp.where(keep, lo, pltpu.roll(hi, shift, 1))
            a[i + s] = jnp.where(keep, pltpu.roll(lo, LANES - shift, 1), hi)
        s //= 2
    return a


def _state_row_pitch(rows):
    assert rows % 8 == 0
    return rows + 4


def _mixer_kernel(*refs, tm):
    x_ref = refs[0]
    rstate, sv, sw = refs[19:22]

    @pl.when(pl.program_id(1) == 0)
    def _():
        rstate[...] = jnp.zeros_like(rstate)
        sv[...] = jnp.zeros_like(sv)
        sw[...] = jnp.zeros_like(sw)

    for base in range(0, x_ref.shape[1], tm):
        _mixer_sub_tile(base, tm, *refs)


def _mixer_sub_tile(base, tm, x_ref, pos_ref, invf_ref, gpre_ref, win_ref, dec_ref, xi_ref, zeta_ref, cd_ref,
                    wret_ref, m_ref, wst_ref, cout_ref, lama_ref, lamb_ref, wglu_ref, wout_ref,
                    gpost_ref, o_ref,
                    rstate, sv, sw, u_scr, q_scr, k_scr, v_scr, sg_scr, gated_scr, ussm_scr, uc_scr,
                    s_scr, ssw_scr, xin_scr, yc_scr, ynat_scr):
    n_chunks = tm // RET_CHUNK
    rows = tm // SSM_CHUNK
    pitch = _state_row_pitch(rows)
    qkw = RET_HEADS * RET_QK_DIM
    vw = RET_HEADS * RET_V_DIM
    ssm_w = SSM_GROUPS * SSM_GROUP_CH
    lane_tiles = ssm_w // LANES
    halves = SSM_CHUNK // 8
    d_model = x_ref.shape[2]
    hq = RET_QK_DIM // 2
    o_v = 2 * qkw
    o_gr = o_v + vw
    o_ssm = o_gr + vw
    o_ga = o_ssm + ssm_w
    o_gb = o_ga + d_model

    def tile_rows(rs):
        return slice(base + rs.start, base + rs.stop)

    def proj(lo, width):
        return _dot(u_scr[...], win_ref[:, lo:lo + width])

    lane_block = lax.broadcasted_iota(jnp.int32, (rows, LANES), 1) // SSM_GROUP_CH


    row_halves = [slice(0, tm // 2), slice(tm // 2, tm)]
    for rs in row_halves:
        x = x_ref[0, tile_rows(rs), :]
        u_scr[rs, :] = (x * _rms_scale(x) * gpre_ref[...]).astype(BF16)

    for rs in row_halves:
        u_ssm = _dot(u_scr[rs, :], win_ref[:, o_ssm:o_ssm + ssm_w])
        for qt in range(lane_tiles):
            ussm_scr[qt, rs, :] = u_ssm[:, qt * LANES:(qt + 1) * LANES]

    q = proj(0, qkw)
    k = proj(qkw, qkw)

    for qt in range(lane_tiles):
        for half in range(halves):
            pieces = [ussm_scr[qt, pl.ds(8 * half + j, rows, stride=SSM_CHUNK), :] for j in range(8)]
            t = _transpose8(pieces, lane_block)
            for gl in range(GROUPS_PER_LANE_TILE):
                uc_scr[qt * GROUPS_PER_LANE_TILE + gl, :, half * LANES:(half + 1) * LANES] = t[gl].astype(BF16)

    half_tm = tm // 2
    pos_rows = pos_ref[0, pl.program_id(1), base // LANES:(base + tm) // LANES, :]
    half_rows = half_tm // LANES
    pos_lo = jnp.concatenate([pos_rows[r:r + 1, :] for r in range(half_rows)], axis=1)
    pos_hi = jnp.concatenate([pos_rows[r:r + 1, :] for r in range(half_rows, 2 * half_rows)], axis=1)
    upper = lax.broadcasted_iota(jnp.int32, (LANES, half_tm), 0) < hq
    ang_t = invf_ref[...] * jnp.where(upper, pos_lo, pos_hi)
    cs = jnp.cos(ang_t).T
    sn = jnp.sin(ang_t).T
    first = lax.broadcasted_iota(jnp.int32, (half_tm, LANES), 1) < hq
    cs_sw = pltpu.roll(cs, hq, 1)
    sn_sw = pltpu.roll(sn, hq, 1)
    cos2 = jnp.concatenate([jnp.where(first, cs, cs_sw), jnp.where(first, cs_sw, cs)], axis=0)
    sin2 = jnp.concatenate([jnp.where(first, -sn, sn_sw), jnp.where(first, -sn_sw, sn)], axis=0)
    kscale = RET_QK_DIM ** -0.5
    for h in range(RET_HEADS):
        sl = slice(h * RET_QK_DIM, (h + 1) * RET_QK_DIM)
        qh = q[:, sl]
        kh = k[:, sl]
        q_scr[:, sl] = (qh * cos2 + pltpu.roll(qh, hq, 1) * sin2).astype(BF16)
        k_scr[:, sl] = (kh * cos2 + pltpu.roll(kh, hq, 1) * sin2) * kscale

    for g in range(SSM_GROUPS):
        s_g = _dot(uc_scr[g], wst_ref[g])
        s_scr[g * pitch:g * pitch + rows, :] = s_g
        ssw_scr[g * pitch:g * pitch + rows, :] = pltpu.roll(s_g, SSM_STATE, 1)

    v_scr[...] = proj(o_v, vw).astype(BF16)
    g_ret = proj(o_gr, vw)
    sg_scr[...] = (g_ret * _sigmoid(g_ret)).astype(BF16)

    gate_a = _sigmoid(proj(o_ga, d_model)).astype(BF16)
    gate_b = _sigmoid(proj(o_gb, d_model)).astype(BF16)

    lam_a = lama_ref[...]
    lam_b = lamb_ref[...]
    v = sv[...]
    w = sw[...]
    for n in range(rows):
        rsel = pl.ds(n, SSM_GROUPS, stride=pitch)
        xin_scr[rsel, :] = v
        v, w = (lam_a * v + lam_b * w + s_scr[rsel, :],
                lam_a * w - lam_b * v + ssw_scr[rsel, :])
    sv[...] = v
    sw[...] = w

    for g in range(SSM_GROUPS):
        x_in = xin_scr[g * pitch:g * pitch + rows, :].astype(BF16)
        yc_scr[g] = _dot(uc_scr[g], m_ref[g]) + _dot(x_in, cout_ref[g])

    for c in range(n_chunks):
        rs = slice(c * RET_CHUNK, (c + 1) * RET_CHUNK)
        qsl = [slice(h * RET_QK_DIM, (h + 1) * RET_QK_DIM) for h in range(RET_HEADS)]
        vsl = [slice(h * RET_V_DIM, (h + 1) * RET_V_DIM) for h in range(RET_HEADS)]
        qh = [q_scr[rs, qsl[h]] for h in range(RET_HEADS)]
        kf = [k_scr[rs, qsl[h]] for h in range(RET_HEADS)]
        vh = [v_scr[rs, vsl[h]] for h in range(RET_HEADS)]
        scores = [_dot_nt(qh[h], kf[h].astype(BF16)) * dec_ref[h] for h in range(RET_HEADS)]
        st = [rstate[h] for h in range(RET_HEADS)]
        cross = [_dot(qh[h], st[h].astype(BF16)) * xi_ref[h] for h in range(RET_HEADS)]
        for h in range(RET_HEADS):
            kz_t = (kf[h] * zeta_ref[h]).T.astype(BF16)
            rstate[h] = cd_ref[h] * st[h] + _dot(kz_t, vh[h])
        for h in range(RET_HEADS):
            ret = _dot(scores[h].astype(BF16), vh[h]) + cross[h]
            dev = ret - jnp.mean(ret, axis=-1, keepdims=True)
            rn = dev * lax.rsqrt(jnp.mean(dev * dev, axis=-1, keepdims=True) + EPS)
            gated_scr[rs, vsl[h]] = (rn * sg_scr[rs, vsl[h]].astype(F32)).astype(BF16)

    for qt in range(lane_tiles):
        for half in range(halves):
            pieces = [yc_scr[qt * GROUPS_PER_LANE_TILE + gl, :, half * LANES:(half + 1) * LANES]
                      for gl in range(GROUPS_PER_LANE_TILE)]
            t = _transpose8(pieces, lane_block)
            for i in range(8):
                ynat_scr[qt, pl.ds(8 * half + i, rows, stride=SSM_CHUNK), :] = t[i]
    z = jnp.concatenate([jax.nn.gelu(ynat_scr[qt]).astype(BF16) for qt in range(lane_tiles)], axis=1)

    glu = _dot(z, wglu_ref[...])
    y_b = glu[:, :d_model] * _sigmoid(glu[:, d_model:])
    y_a = _dot(gated_scr[...], wret_ref[...])

    mix = (gate_a.astype(F32) * y_a + gate_b.astype(F32) * y_b).astype(BF16)
    for rs in row_halves:
        mixed = _dot(mix[rs, :], wout_ref[...])
        o_ref[0, tile_rows(rs), :] = x_ref[0, tile_rows(rs), :] + mixed * _rms_scale(mixed) * gpost_ref[...]


def _ffn_kernel(h_ref, gpre_ref, wg_ref, wu_ref, wd_ref, gpost_ref, o_ref, *, hidden_chunks, row_block):
    for r0 in range(0, h_ref.shape[0], row_block):
        rs = slice(r0, r0 + row_block)
        h = h_ref[rs, :]
        f_in = (h * _rms_scale(h) * gpre_ref[...]).astype(BF16)
        f = None
        for lo, hi in hidden_chunks:
            gate = _dot(f_in, wg_ref[:, lo:hi])
            up = _dot(f_in, wu_ref[:, lo:hi])
            part = _dot((gate * _sigmoid(gate) * up).astype(BF16), wd_ref[lo:hi, :])
            f = part if f is None else f + part
        o_ref[rs, :] = h_ref[rs, :] + f * _rms_scale(f) * gpost_ref[...]


def _resident(shape):
    nd = len(shape)
    return pl.BlockSpec(shape, lambda *_: (0,) * nd, pipeline_mode=pl.Buffered(1))


def _retention_constants():
    c = RET_CHUNK
    log_g = jnp.log(1.0 - 2.0 ** (-5.0 - jnp.arange(RET_HEADS, dtype=F32)))
    idx = jnp.arange(c, dtype=F32)
    diff = idx[:, None] - idx[None, :]
    inner_decay = jnp.where(diff >= 0, jnp.exp(log_g[:, None, None] * jnp.maximum(diff, 0.0)), 0.0)
    zeta = jnp.exp(log_g[:, None] * (c - 1.0 - idx)[None, :])
    xi = jnp.exp(log_g[:, None] * (idx + 1.0)[None, :])
    chunk_decay = jnp.exp(log_g * c)
    zeta_full = jnp.broadcast_to(zeta[:, :, None], (RET_HEADS, c, RET_QK_DIM))
    xi_full = jnp.broadcast_to(xi[:, :, None], (RET_HEADS, c, RET_V_DIM))
    cd_full = jnp.broadcast_to(chunk_decay[:, None, None], (RET_HEADS, 1, RET_V_DIM))
    return inner_decay.astype(F32), xi_full.astype(F32), zeta_full.astype(F32), cd_full.astype(F32)


def _dot_f32(a, b):
    return jnp.dot(a, b, precision=lax.Precision.HIGHEST, preferred_element_type=F32)


def _s5_prep_kernel(*refs):
    for gi in range(refs[0].shape[0]):
        _s5_prep_group(*[r.at[gi] for r in refs])


def _s5_prep_group(arc_ref, aic_ref, arr_ref, air_ref, ldt_ref, btr_ref, bti_ref, ctr_ref, cti_ref,
                   dsk_ref, m_ref, wst_ref, cout_ref, lama_ref, lamb_ref):
    tc = SSM_CHUNK
    dt = jnp.exp(ldt_ref[...])

    def discretise(a_r, a_i):
        da_r = dt * a_r
        da_i = dt * a_i
        mag = jnp.exp(da_r)
        l_r = mag * jnp.cos(da_i)
        l_i = mag * jnp.sin(da_i)
        den = a_r * a_r + a_i * a_i
        num_r = l_r - 1.0
        return l_r, l_i, (num_r * a_r + l_i * a_i) / den, (l_i * a_r - num_r * a_i) / den

    lc_r, lc_i, _, _ = discretise(arc_ref[...], aic_ref[...])
    lr_r, lr_i, f_r, f_i = discretise(arr_ref[...], air_ref[...])
    bbt_r = f_r * btr_ref[...] - f_i * bti_ref[...]
    bbt_i = f_r * bti_ref[...] + f_i * btr_ref[...]

    lane_d = lax.broadcasted_iota(jnp.int32, (SSM_STATE, SSM_CHUNK_W), 1) // SSM_GROUP_CH
    pc_r = jnp.ones((SSM_STATE, 1), F32)
    pc_i = jnp.zeros((SSM_STATE, 1), F32)
    pw_r = jnp.zeros((SSM_STATE, SSM_CHUNK_W), F32)
    pw_i = jnp.zeros((SSM_STATE, SSM_CHUNK_W), F32)
    for d in range(tc):
        sel = lane_d == d
        pw_r = jnp.where(sel, pc_r, pw_r)
        pw_i = jnp.where(sel, pc_i, pw_i)
        pc_r, pc_i = pc_r * lc_r - pc_i * lc_i, pc_r * lc_i + pc_i * lc_r
    ct_r = ctr_ref[...]
    ct_i = cti_ref[...]

    r0 = (_dot_f32(bbt_r, pw_r * ct_r - pw_i * ct_i)
          - _dot_f32(bbt_i, pw_i * ct_r + pw_r * ct_i))
    row = lax.broadcasted_iota(jnp.int32, (SSM_GROUP_CH, SSM_CHUNK_W), 0)
    lane = lax.broadcasted_iota(jnp.int32, (SSM_GROUP_CH, SSM_CHUNK_W), 1)
    r0 = r0 + jnp.where(row == lane, dsk_ref[...], 0.0)
    for j in range(tc):
        blk = r0 if j == 0 else jnp.where(lane >= SSM_GROUP_CH * j, pltpu.roll(r0, SSM_GROUP_CH * j, 1), 0.0)
        m_ref[j * SSM_GROUP_CH:(j + 1) * SSM_GROUP_CH, :] = blk.astype(BF16)

    p1_r = pw_r * lc_r - pw_i * lc_i
    p1_i = pw_r * lc_i + pw_i * lc_r
    cout_ref[0:SSM_STATE, :] = (p1_r * ct_r - p1_i * ct_i).astype(BF16)
    cout_ref[SSM_STATE:2 * SSM_STATE, :] = (-(p1_i * ct_r + p1_r * ct_i)).astype(BF16)

    pr_r = jnp.ones((1, SSM_STATE), F32)
    pr_i = jnp.zeros((1, SSM_STATE), F32)
    powers = []
    for d in range(tc + 1):
        powers.append((pr_r, pr_i))
        pr_r, pr_i = pr_r * lr_r - pr_i * lr_i, pr_r * lr_i + pr_i * lr_r
    for j in range(tc):
        q_r, q_i = powers[tc - 1 - j]
        blk = jnp.concatenate([bbt_r * q_r - bbt_i * q_i, bbt_i * q_r + bbt_r * q_i], axis=1)
        wst_ref[j * SSM_GROUP_CH:(j + 1) * SSM_GROUP_CH, :] = blk.astype(BF16)
    q_r, q_i = powers[tc]
    lama_ref[...] = jnp.concatenate([q_r, q_r], axis=1)
    lamb_ref[...] = jnp.concatenate([-q_i, q_i], axis=1)


def _s5_prep_call(a_re, a_im, log_dt, b_re, b_im, c_re, c_im, d_skip):
    g, p, c = SSM_GROUPS, SSM_STATE, SSM_GROUP_CH
    tile_c = lambda m: jnp.tile(jnp.swapaxes(m, 1, 2), (1, 1, SSM_CHUNK))
    operands = [a_re.reshape(g, p, 1), a_im.reshape(g, p, 1), a_re.reshape(g, 1, p), a_im.reshape(g, 1, p),
                log_dt.reshape(g, 1, 1), jnp.swapaxes(b_re, 1, 2), jnp.swapaxes(b_im, 1, 2),
                tile_c(c_re), tile_c(c_im), jnp.tile(d_skip.reshape(g, 1, c), (1, 1, SSM_CHUNK))]
    out_shapes = [((SSM_CHUNK_W, SSM_CHUNK_W), BF16), ((SSM_CHUNK_W, 2 * p), BF16), ((2 * p, SSM_CHUNK_W), BF16),
                  ((1, 2 * p), F32), ((1, 2 * p), F32)]
    spec = lambda shp: pl.BlockSpec((PREP_GROUPS_PER_STEP,) + tuple(shp), lambda i: (i, 0, 0))
    m, wst, cout, lam_a, lam_b = pl.pallas_call(
        _s5_prep_kernel,
        grid=(g // PREP_GROUPS_PER_STEP,),
        in_specs=[spec(o.shape[1:]) for o in operands],
        out_specs=[spec(shp) for shp, _ in out_shapes],
        out_shape=[jax.ShapeDtypeStruct((g,) + shp, dt) for shp, dt in out_shapes],
        compiler_params=pltpu.CompilerParams(dimension_semantics=("arbitrary",)),
        name="s5_prep",
    )(*operands)
    return m, wst, cout, lam_a.reshape(g, 2 * p), lam_b.reshape(g, 2 * p)


def _mixer_call(x, pos_f, invf, gpre, win, ret_consts, wret, s5_ops, wglu, wout, gpost):
    bsz, seq, d_model = x.shape
    tile = min(MIXER_TILE, seq)
    tm = min(MIXER_SUB_TILE, tile)
    assert seq % tile == 0 and tile % tm == 0 and tm % RET_CHUNK == 0
    rows = tm // SSM_CHUNK
    pitch = _state_row_pitch(rows)
    inner_decay, xi_full, zeta_full, cd_full = ret_consts
    m, wst, cout, lam_a, lam_b = s5_ops
    qkw = RET_HEADS * RET_QK_DIM
    vw = RET_HEADS * RET_V_DIM
    ssm_w = SSM_GROUPS * SSM_GROUP_CH
    consts = [invf, gpre, win, inner_decay, xi_full, zeta_full, cd_full, wret, m, wst, cout,
              lam_a, lam_b, wglu, wout, gpost]
    in_specs = [pl.BlockSpec((1, tile, d_model), lambda b, t: (b, t, 0)),
                pl.BlockSpec((1, seq // tile, tile // LANES, LANES), lambda b, t: (b, 0, 0, 0))]
    in_specs += [_resident(c.shape) for c in consts]
    scratch = [
        pltpu.VMEM((RET_HEADS, RET_QK_DIM, RET_V_DIM), F32),
        pltpu.VMEM((SSM_GROUPS, 2 * SSM_STATE), F32),
        pltpu.VMEM((SSM_GROUPS, 2 * SSM_STATE), F32),
        pltpu.VMEM((tm, d_model), BF16),
        pltpu.VMEM((tm, qkw), BF16),
        pltpu.VMEM((tm, qkw), F32),
        pltpu.VMEM((tm, vw), BF16),
        pltpu.VMEM((tm, vw), BF16),
        pltpu.VMEM((tm, vw), BF16),
        pltpu.VMEM((ssm_w // LANES, tm, LANES), F32),
        pltpu.VMEM((SSM_GROUPS, rows, SSM_CHUNK_W), BF16),
        pltpu.VMEM((SSM_GROUPS * pitch, 2 * SSM_STATE), F32),
        pltpu.VMEM((SSM_GROUPS * pitch, 2 * SSM_STATE), F32),
        pltpu.VMEM((SSM_GROUPS * pitch, 2 * SSM_STATE), F32),
        pltpu.VMEM((SSM_GROUPS, rows, SSM_CHUNK_W), F32),
        pltpu.VMEM((ssm_w // LANES, tm, LANES), F32),
    ]
    return pl.pallas_call(
        functools.partial(_mixer_kernel, tm=tm),
        grid=(bsz, seq // tile),
        in_specs=in_specs,
        out_specs=pl.BlockSpec((1, tile, d_model), lambda b, t: (b, t, 0)),
        out_shape=jax.ShapeDtypeStruct(x.shape, F32),
        scratch_shapes=scratch,
        compiler_params=pltpu.CompilerParams(
            dimension_semantics=("arbitrary", "arbitrary"),
            vmem_limit_bytes=VMEM_LIMIT_BYTES),
        name="mixer",
    )(x, pos_f.reshape(bsz, seq // tile, tile // LANES, LANES), *consts)


def _ffn_call(h, gpre, wg, wu, wd, gpost):
    n_tok, d_model = h.shape
    hidden = wg.shape[1]
    tm = min(FFN_TILE, n_tok)
    assert n_tok % tm == 0
    step = 1024
    hidden_chunks = tuple((lo, min(lo + step, hidden)) for lo in range(0, hidden, step))
    consts = [gpre, wg, wu, wd, gpost]
    return pl.pallas_call(
        functools.partial(_ffn_kernel, hidden_chunks=hidden_chunks, row_block=min(FFN_ROW_BLOCK, tm)),
        grid=(n_tok // tm,),
        in_specs=[pl.BlockSpec((tm, d_model), lambda i: (i, 0))] + [_resident(c.shape) for c in consts],
        out_specs=pl.BlockSpec((tm, d_model), lambda i: (i, 0)),
        out_shape=jax.ShapeDtypeStruct(h.shape, F32),
        compiler_params=pltpu.CompilerParams(
            dimension_semantics=("arbitrary",),
            vmem_limit_bytes=VMEM_LIMIT_BYTES),
        name="ffn",
    )(h, *consts)


def kernel(x, positions, mix_pre_norm, w_in, ssm_a_re, ssm_a_im, ssm_log_dt, ssm_b_re, ssm_b_im, ssm_c_re, ssm_c_im, ssm_d, w_glu_val, w_glu_gate, w_ret_up, w_out, mix_post_norm, ffn_pre_norm, w_ffn_gate, w_ffn_up, w_ffn_down, ffn_post_norm):
    bsz, seq, d_model = x.shape
    depth = w_in.shape[0]
    pos_f = positions.astype(F32)
    inv_freq = ROPE_BASE ** (-jnp.arange(0, RET_QK_DIM, 2, dtype=F32) / RET_QK_DIM)
    invf = jnp.concatenate([inv_freq, inv_freq]).reshape(RET_QK_DIM, 1)
    ret_consts = _retention_constants()
    h = x
    for layer in range(depth):
        s5_ops = _s5_prep_call(ssm_a_re[layer], ssm_a_im[layer], ssm_log_dt[layer], ssm_b_re[layer],
                               ssm_b_im[layer], ssm_c_re[layer], ssm_c_im[layer], ssm_d[layer])
        wglu = jnp.concatenate([w_glu_val[layer], w_glu_gate[layer]], axis=1).astype(BF16)
        h = _mixer_call(h, pos_f, invf, mix_pre_norm[layer].reshape(1, d_model), w_in[layer].astype(BF16),
                        ret_consts, w_ret_up[layer].astype(BF16), s5_ops, wglu, w_out[layer].astype(BF16),
                        mix_post_norm[layer].reshape(1, d_model))
        h = _ffn_call(h.reshape(bsz * seq, d_model), ffn_pre_norm[layer].reshape(1, d_model),
                      w_ffn_gate[layer].astype(BF16), w_ffn_up[layer].astype(BF16),
                      w_ffn_down[layer].astype(BF16), ffn_post_norm[layer].reshape(1, d_model)
                      ).reshape(bsz, seq, d_model)
    return h
```

```python
import functools

import jax
import jax.numpy as jnp
from jax import lax
from jax.experimental import pallas as pl
from jax.experimental.pallas import tpu as pltpu

F32 = jnp.float32
BF16 = jnp.bfloat16

EPS = 1e-6
ROPE_BASE = 10000.0
RET_HEADS = 4
RET_QK_DIM = 128
RET_V_DIM = 256
RET_CHUNK = 128
SSM_GROUP_CH = 16
SSM_GROUPS = 32
SSM_STATE = 64
SSM_CHUNK = 8
SSM_CHUNK_W = SSM_CHUNK * SSM_GROUP_CH
LANES = 128
GROUPS_PER_LANE_TILE = LANES // SSM_GROUP_CH
VMEM_LIMIT_BYTES = 58 * 1024 * 1024

MIXER_TILE = 1024
MIXER_SUB_TILE = 512
FFN_TILE = 1024
FFN_ROW_BLOCK = 256
PREP_GROUPS_PER_STEP = 4


def _dot(a, b):
    return jnp.dot(a, b, preferred_element_type=F32)


def _dot_nt(a, b):
    return lax.dot_general(a, b, (((1,), (1,)), ((), ())), preferred_element_type=F32)


def _sigmoid(x):
    return 0.5 * jnp.tanh(0.5 * x) + 0.5


def _rms_scale(x):
    return lax.rsqrt(jnp.mean(x * x, axis=-1, keepdims=True) + EPS)


def _transpose8(pieces, lane_block):
    a = list(pieces)
    s = 4
    while s >= 1:
        keep = (lane_block & s) == 0
        shift = SSM_GROUP_CH * s
        for i in range(8):
            if i & s:
                continue
            lo, hi = a[i], a[i + s]
            a[i] = jnp.where(keep, lo, pltpu.roll(hi, shift, 1))
            a[i + s] = jnp.where(keep, pltpu.roll(lo, LANES - shift, 1), hi)
        s //= 2
    return a


def _state_row_pitch(rows):
    assert rows % 8 == 0
    return rows + 4


def _mixer_kernel(*refs, tm):
    x_ref = refs[0]
    rstate, sv, sw = refs[19:22]

    @pl.when(pl.program_id(1) == 0)
    def _():
        rstate[...] = jnp.zeros_like(rstate)
        sv[...] = jnp.zeros_like(sv)
        sw[...] = jnp.zeros_like(sw)

    for base in range(0, x_ref.shape[1], tm):
        _mixer_sub_tile(base, tm, *refs)


def _mixer_sub_tile(base, tm, x_ref, pos_ref, invf_ref, gpre_ref, win_ref, dec_ref, xi_ref, zeta_ref, cd_ref,
                    wret_ref, m_ref, wst_ref, cout_ref, lama_ref, lamb_ref, wglu_ref, wout_ref,
                    gpost_ref, o_ref,
                    rstate, sv, sw, u_scr, q_scr, k_scr, v_scr, sg_scr, gated_scr, ussm_scr, uc_scr,
                    s_scr, ssw_scr, xin_scr, yc_scr, ynat_scr):
    n_chunks = tm // RET_CHUNK
    rows = tm // SSM_CHUNK
    pitch = _state_row_pitch(rows)
    qkw = RET_HEADS * RET_QK_DIM
    vw = RET_HEADS * RET_V_DIM
    ssm_w = SSM_GROUPS * SSM_GROUP_CH
    lane_tiles = ssm_w // LANES
    halves = SSM_CHUNK // 8
    d_model = x_ref.shape[2]
    hq = RET_QK_DIM // 2
    o_v = 2 * qkw
    o_gr = o_v + vw
    o_ssm = o_gr + vw
    o_ga = o_ssm + ssm_w
    o_gb = o_ga + d_model

    def tile_rows(rs):
        return slice(base + rs.start, base + rs.stop)

    def proj(lo, width):
        return _dot(u_scr[...], win_ref[:, lo:lo + width])

    lane_block = lax.broadcasted_iota(jnp.int32, (rows, LANES), 1) // SSM_GROUP_CH


    row_halves = [slice(0, tm // 2), slice(tm // 2, tm)]
    for rs in row_halves:
        x = x_ref[0, tile_rows(rs), :]
        u_scr[rs, :] = (x * _rms_scale(x) * gpre_ref[...]).astype(BF16)

    for rs in row_halves:
        u_ssm = _dot(u_scr[rs, :], win_ref[:, o_ssm:o_ssm + ssm_w])
        for qt in range(lane_tiles):
            ussm_scr[qt, rs, :] = u_ssm[:, qt * LANES:(qt + 1) * LANES]

    q = proj(0, qkw)
    k = proj(qkw, qkw)

    for qt in range(lane_tiles):
        for half in range(halves):
            pieces = [ussm_scr[qt, pl.ds(8 * half + j, rows, stride=SSM_CHUNK), :] for j in range(8)]
            t = _transpose8(pieces, lane_block)
            for gl in range(GROUPS_PER_LANE_TILE):
                uc_scr[qt * GROUPS_PER_LANE_TILE + gl, :, half * LANES:(half + 1) * LANES] = t[gl].astype(BF16)

    half_tm = tm // 2
    pos_rows = pos_ref[0, pl.program_id(1), base // LANES:(base + tm) // LANES, :]
    half_rows = half_tm // LANES
    pos_lo = jnp.concatenate([pos_rows[r:r + 1, :] for r in range(half_rows)], axis=1)
    pos_hi = jnp.concatenate([pos_rows[r:r + 1, :] for r in range(half_rows, 2 * half_rows)], axis=1)
    upper = lax.broadcasted_iota(jnp.int32, (LANES, half_tm), 0) < hq
    ang_t = invf_ref[...] * jnp.where(upper, pos_lo, pos_hi)
    cs = jnp.cos(ang_t).T
    sn = jnp.sin(ang_t).T
    first = lax.broadcasted_iota(jnp.int32, (half_tm, LANES), 1) < hq
    cs_sw = pltpu.roll(cs, hq, 1)
    sn_sw = pltpu.roll(sn, hq, 1)
    cos2 = jnp.concatenate([jnp.where(first, cs, cs_sw), jnp.where(first, cs_sw, cs)], axis=0)
    sin2 = jnp.concatenate([jnp.where(first, -sn, sn_sw), jnp.where(first, -sn_sw, sn)], axis=0)
    kscale = RET_QK_DIM ** -0.5
    for h in range(RET_HEADS):
        sl = slice(h * RET_QK_DIM, (h + 1) * RET_QK_DIM)
        qh = q[:, sl]
        kh = k[:, sl]
        q_scr[:, sl] = (qh * cos2 + pltpu.roll(qh, hq, 1) * sin2).astype(BF16)
        k_scr[:, sl] = (kh * cos2 + pltpu.roll(kh, hq, 1) * sin2) * kscale

    for g in range(SSM_GROUPS):
        s_g = _dot(uc_scr[g], wst_ref[g])
        s_scr[g * pitch:g * pitch + rows, :] = s_g
        ssw_scr[g * pitch:g * pitch + rows, :] = pltpu.roll(s_g, SSM_STATE, 1)

    v_scr[...] = proj(o_v, vw).astype(BF16)
    g_ret = proj(o_gr, vw)
    sg_scr[...] = (g_ret * _sigmoid(g_ret)).astype(BF16)

    gate_a = _sigmoid(proj(o_ga, d_model))
    gate_b = _sigmoid(proj(o_gb, d_model))

    lam_a = lama_ref[...]
    lam_b = lamb_ref[...]
    v = sv[...]
    w = sw[...]
    for n in range(rows):
        rsel = pl.ds(n, SSM_GROUPS, stride=pitch)
        xin_scr[rsel, :] = v
        v, w = (lam_a * v + lam_b * w + s_scr[rsel, :],
                lam_a * w - lam_b * v + ssw_scr[rsel, :])
    sv[...] = v
    sw[...] = w

    for g in range(SSM_GROUPS):
        x_in = xin_scr[g * pitch:g * pitch + rows, :].astype(BF16)
        yc_scr[g] = _dot(uc_scr[g], m_ref[g]) + _dot(x_in, cout_ref[g])

    for c in range(n_chunks):
        rs = slice(c * RET_CHUNK, (c + 1) * RET_CHUNK)
        qsl = [slice(h * RET_QK_DIM, (h + 1) * RET_QK_DIM) for h in range(RET_HEADS)]
        vsl = [slice(h * RET_V_DIM, (h + 1) * RET_V_DIM) for h in range(RET_HEADS)]
        qh = [q_scr[rs, qsl[h]] for h in range(RET_HEADS)]
        kf = [k_scr[rs, qsl[h]] for h in range(RET_HEADS)]
        vh = [v_scr[rs, vsl[h]] for h in range(RET_HEADS)]
        scores = [_dot_nt(qh[h], kf[h].astype(BF16)) * dec_ref[h] for h in range(RET_HEADS)]
        st = [rstate[h] for h in range(RET_HEADS)]
        cross = [_dot(qh[h], st[h].astype(BF16)) * xi_ref[h] for h in range(RET_HEADS)]
        for h in range(RET_HEADS):
            kz_t = (kf[h] * zeta_ref[h]).T.astype(BF16)
            rstate[h] = cd_ref[h] * st[h] + _dot(kz_t, vh[h])
        for h in range(RET_HEADS):
            ret = _dot(scores[h].astype(BF16), vh[h]) + cross[h]
            dev = ret - jnp.mean(ret, axis=-1, keepdims=True)
            rn = dev * lax.rsqrt(jnp.mean(dev * dev, axis=-1, keepdims=True) + EPS)
            gated_scr[rs, vsl[h]] = (rn * sg_scr[rs, vsl[h]].astype(F32)).astype(BF16)

    for qt in range(lane_tiles):
        for half in range(halves):
            pieces = [yc_scr[qt * GROUPS_PER_LANE_TILE + gl, :, half * LANES:(half + 1) * LANES]
                      for gl in range(GROUPS_PER_LANE_TILE)]
            t = _transpose8(pieces, lane_block)
            for i in range(8):
                ynat_scr[qt, pl.ds(8 * half + i, rows, stride=SSM_CHUNK), :] = t[i]
    z = jnp.concatenate([jax.nn.gelu(ynat_scr[qt]).astype(BF16) for qt in range(lane_tiles)], axis=1)

    glu = _dot(z, wglu_ref[...])
    y_b = glu[:, :d_model] * _sigmoid(glu[:, d_model:])
    y_a = _dot(gated_scr[...], wret_ref[...])

    mix = (gate_a * y_a + gate_b * y_b).astype(BF16)
    for rs in row_halves:
        mixed = _dot(mix[rs, :], wout_ref[...])
        o_ref[0, tile_rows(rs), :] = x_ref[0, tile_rows(rs), :] + mixed * _rms_scale(mixed) * gpost_ref[...]


def _ffn_kernel(h_ref, gpre_ref, wg_ref, wu_ref, wd_ref, gpost_ref, o_ref, *, hidden_chunks, row_block):
    for r0 in range(0, h_ref.shape[0], row_block):
        rs = slice(r0, r0 + row_block)
        h = h_ref[rs, :]
        f_in = (h * _rms_scale(h) * gpre_ref[...]).astype(BF16)
        f = None
        for lo, hi in hidden_chunks:
            gate = _dot(f_in, wg_ref[:, lo:hi])
            up = _dot(f_in, wu_ref[:, lo:hi])
            part = _dot((gate * _sigmoid(gate) * up).astype(BF16), wd_ref[lo:hi, :])
            f = part if f is None else f + part
        o_ref[rs, :] = h_ref[rs, :] + f * _rms_scale(f) * gpost_ref[...]


def _resident(shape):
    nd = len(shape)
    return pl.BlockSpec(shape, lambda *_: (0,) * nd, pipeline_mode=pl.Buffered(1))


def _retention_constants():
    c = RET_CHUNK
    log_g = jnp.log(1.0 - 2.0 ** (-5.0 - jnp.arange(RET_HEADS, dtype=F32)))
    idx = jnp.arange(c, dtype=F32)
    diff = idx[:, None] - idx[None, :]
    inner_decay = jnp.where(diff >= 0, jnp.exp(log_g[:, None, None] * jnp.maximum(diff, 0.0)), 0.0)
    zeta = jnp.exp(log_g[:, None] * (c - 1.0 - idx)[None, :])
    xi = jnp.exp(log_g[:, None] * (idx + 1.0)[None, :])
    chunk_decay = jnp.exp(log_g * c)
    zeta_full = jnp.broadcast_to(zeta[:, :, None], (RET_HEADS, c, RET_QK_DIM))
    xi_full = jnp.broadcast_to(xi[:, :, None], (RET_HEADS, c, RET_V_DIM))
    cd_full = jnp.broadcast_to(chunk_decay[:, None, None], (RET_HEADS, 1, RET_V_DIM))
    return inner_decay.astype(F32), xi_full.astype(F32), zeta_full.astype(F32), cd_full.astype(F32)


def _dot_f32(a, b):
    return jnp.dot(a, b, precision=lax.Precision.HIGHEST, preferred_element_type=F32)


def _s5_prep_kernel(*refs):
    for gi in range(refs[0].shape[0]):
        _s5_prep_group(*[r.at[gi] for r in refs])


def _s5_prep_group(arc_ref, aic_ref, arr_ref, air_ref, ldt_ref, btr_ref, bti_ref, ctr_ref, cti_ref,
                   dsk_ref, m_ref, wst_ref, cout_ref, lama_ref, lamb_ref):
    tc = SSM_CHUNK
    dt = jnp.exp(ldt_ref[...])

    def discretise(a_r, a_i):
        da_r = dt * a_r
        da_i = dt * a_i
        mag = jnp.exp(da_r)
        l_r = mag * jnp.cos(da_i)
        l_i = mag * jnp.sin(da_i)
        den = a_r * a_r + a_i * a_i
        num_r = l_r - 1.0
        return l_r, l_i, (num_r * a_r + l_i * a_i) / den, (l_i * a_r - num_r * a_i) / den

    lc_r, lc_i, _, _ = discretise(arc_ref[...], aic_ref[...])
    lr_r, lr_i, f_r, f_i = discretise(arr_ref[...], air_ref[...])
    bbt_r = f_r * btr_ref[...] - f_i * bti_ref[...]
    bbt_i = f_r * bti_ref[...] + f_i * btr_ref[...]

    lane_d = lax.broadcasted_iota(jnp.int32, (SSM_STATE, SSM_CHUNK_W), 1) // SSM_GROUP_CH
    pc_r = jnp.ones((SSM_STATE, 1), F32)
    pc_i = jnp.zeros((SSM_STATE, 1), F32)
    pw_r = jnp.zeros((SSM_STATE, SSM_CHUNK_W), F32)
    pw_i = jnp.zeros((SSM_STATE, SSM_CHUNK_W), F32)
    for d in range(tc):
        sel = lane_d == d
        pw_r = jnp.where(sel, pc_r, pw_r)
        pw_i = jnp.where(sel, pc_i, pw_i)
        pc_r, pc_i = pc_r * lc_r - pc_i * lc_i, pc_r * lc_i + pc_i * lc_r
    ct_r = ctr_ref[...]
    ct_i = cti_ref[...]

    r0 = (_dot_f32(bbt_r, pw_r * ct_r - pw_i * ct_i)
          - _dot_f32(bbt_i, pw_i * ct_r + pw_r * ct_i))
    row = lax.broadcasted_iota(jnp.int32, (SSM_GROUP_CH, SSM_CHUNK_W), 0)
    lane = lax.broadcasted_iota(jnp.int32, (SSM_GROUP_CH, SSM_CHUNK_W), 1)
    r0 = r0 + jnp.where(row == lane, dsk_ref[...], 0.0)
    for j in range(tc):
        blk = r0 if j == 0 else jnp.where(lane >= SSM_GROUP_CH * j, pltpu.roll(r0, SSM_GROUP_CH * j, 1), 0.0)
        m_ref[j * SSM_GROUP_CH:(j + 1) * SSM_GROUP_CH, :] = blk.astype(BF16)

    p1_r = pw_r * lc_r - pw_i * lc_i
    p1_i = pw_r * lc_i + pw_i * lc_r
    cout_ref[0:SSM_STATE, :] = (p1_r * ct_r - p1_i * ct_i).astype(BF16)
    cout_ref[SSM_STATE:2 * SSM_STATE, :] = (-(p1_i * ct_r + p1_r * ct_i)).astype(BF16)

    pr_r = jnp.ones((1, SSM_STATE), F32)
    pr_i = jnp.zeros((1, SSM_STATE), F32)
    powers = []
    for d in range(tc + 1):
        powers.append((pr_r, pr_i))
        pr_r, pr_i = pr_r * lr_r - pr_i * lr_i, pr_r * lr_i + pr_i * lr_r
    for j in range(tc):
        q_r, q_i = powers[tc - 1 - j]
        blk = jnp.concatenate([bbt_r * q_r - bbt_i * q_i, bbt_i * q_r + bbt_r * q_i], axis=1)
        wst_ref[j * SSM_GROUP_CH:(j + 1) * SSM_GROUP_CH, :] = blk.astype(BF16)
    q_r, q_i = powers[tc]
    lama_ref[...] = jnp.concatenate([q_r, q_r], axis=1)
    lamb_ref[...] = jnp.concatenate([-q_i, q_i], axis=1)


def _s5_prep_call(a_re, a_im, log_dt, b_re, b_im, c_re, c_im, d_skip):
    g, p, c = SSM_GROUPS, SSM_STATE, SSM_GROUP_CH
    tile_c = lambda m: jnp.tile(jnp.swapaxes(m, 1, 2), (1, 1, SSM_CHUNK))
    operands = [a_re.reshape(g, p, 1), a_im.reshape(g, p, 1), a_re.reshape(g, 1, p), a_im.reshape(g, 1, p),
                log_dt.reshape(g, 1, 1), jnp.swapaxes(b_re, 1, 2), jnp.swapaxes(b_im, 1, 2),
                tile_c(c_re), tile_c(c_im), jnp.tile(d_skip.reshape(g, 1, c), (1, 1, SSM_CHUNK))]
    out_shapes = [((SSM_CHUNK_W, SSM_CHUNK_W), BF16), ((SSM_CHUNK_W, 2 * p), BF16), ((2 * p, SSM_CHUNK_W), BF16),
                  ((1, 2 * p), F32), ((1, 2 * p), F32)]
    spec = lambda shp: pl.BlockSpec((PREP_GROUPS_PER_STEP,) + tuple(shp), lambda i: (i, 0, 0))
    m, wst, cout, lam_a, lam_b = pl.pallas_call(
        _s5_prep_kernel,
        grid=(g // PREP_GROUPS_PER_STEP,),
        in_specs=[spec(o.shape[1:]) for o in operands],
        out_specs=[spec(shp) for shp, _ in out_shapes],
        out_shape=[jax.ShapeDtypeStruct((g,) + shp, dt) for shp, dt in out_shapes],
        compiler_params=pltpu.CompilerParams(dimension_semantics=("arbitrary",)),
        name="s5_prep",
    )(*operands)
    return m, wst, cout, lam_a.reshape(g, 2 * p), lam_b.reshape(g, 2 * p)


def _mixer_call(x, pos_f, invf, gpre, win, ret_consts, wret, s5_ops, wglu, wout, gpost):
    bsz, seq, d_model = x.shape
    tile = min(MIXER_TILE, seq)
    tm = min(MIXER_SUB_TILE, tile)
    assert seq % tile == 0 and tile % tm == 0 and tm % RET_CHUNK == 0
    rows = tm // SSM_CHUNK
    pitch = _state_row_pitch(rows)
    inner_decay, xi_full, zeta_full, cd_full = ret_consts
    m, wst, cout, lam_a, lam_b = s5_ops
    qkw = RET_HEADS * RET_QK_DIM
    vw = RET_HEADS * RET_V_DIM
    ssm_w = SSM_GROUPS * SSM_GROUP_CH
    consts = [invf, gpre, win, inner_decay, xi_full, zeta_full, cd_full, wret, m, wst, cout,
              lam_a, lam_b, wglu, wout, gpost]
    in_specs = [pl.BlockSpec((1, tile, d_model), lambda b, t: (b, t, 0)),
                pl.BlockSpec((1, seq // tile, tile // LANES, LANES), lambda b, t: (b, 0, 0, 0))]
    in_specs += [_resident(c.shape) for c in consts]
    scratch = [
        pltpu.VMEM((RET_HEADS, RET_QK_DIM, RET_V_DIM), F32),
        pltpu.VMEM((SSM_GROUPS, 2 * SSM_STATE), F32),
        pltpu.VMEM((SSM_GROUPS, 2 * SSM_STATE), F32),
        pltpu.VMEM((tm, d_model), BF16),
        pltpu.VMEM((tm, qkw), BF16),
        pltpu.VMEM((tm, qkw), F32),
        pltpu.VMEM((tm, vw), BF16),
        pltpu.VMEM((tm, vw), BF16),
        pltpu.VMEM((tm, vw), BF16),
        pltpu.VMEM((ssm_w // LANES, tm, LANES), F32),
        pltpu.VMEM((SSM_GROUPS, rows, SSM_CHUNK_W), BF16),
        pltpu.VMEM((SSM_GROUPS * pitch, 2 * SSM_STATE), F32),
        pltpu.VMEM((SSM_GROUPS * pitch, 2 * SSM_STATE), F32),
        pltpu.VMEM((SSM_GROUPS * pitch, 2 * SSM_STATE), F32),
        pltpu.VMEM((SSM_GROUPS, rows, SSM_CHUNK_W), F32),
        pltpu.VMEM((ssm_w // LANES, tm, LANES), F32),
    ]
    return pl.pallas_call(
        functools.partial(_mixer_kernel, tm=tm),
        grid=(bsz, seq // tile),
        in_specs=in_specs,
        out_specs=pl.BlockSpec((1, tile, d_model), lambda b, t: (b, t, 0)),
        out_shape=jax.ShapeDtypeStruct(x.shape, F32),
        scratch_shapes=scratch,
        compiler_params=pltpu.CompilerParams(
            dimension_semantics=("arbitrary", "arbitrary"),
            vmem_limit_bytes=VMEM_LIMIT_BYTES),
        name="mixer",
    )(x, pos_f.reshape(bsz, seq // tile, tile // LANES, LANES), *consts)


def _ffn_call(h, gpre, wg, wu, wd, gpost):
    n_tok, d_model = h.shape
    hidden = wg.shape[1]
    tm = min(FFN_TILE, n_tok)
    assert n_tok % tm == 0
    step = 1024
    hidden_chunks = tuple((lo, min(lo + step, hidden)) for lo in range(0, hidden, step))
    consts = [gpre, wg, wu, wd, gpost]
    body = functools.partial(_ffn_kernel, hidden_chunks=hidden_chunks, row_block=min(FFN_ROW_BLOCK, tm))

    def outer(h_hbm, gpre_ref, wg_ref, wu_ref, wd_ref, gpost_ref, o_hbm):
        def step(h_ref, o_ref):
            body(h_ref, gpre_ref, wg_ref, wu_ref, wd_ref, gpost_ref, o_ref)

        pltpu.emit_pipeline(
            step,
            grid=(n_tok // tm,),
            in_specs=[pl.BlockSpec((tm, d_model), lambda i: (i, 0), pipeline_mode=pl.Buffered(3))],
            out_specs=[pl.BlockSpec((tm, d_model), lambda i: (i, 0))],
        )(h_hbm, o_hbm)

    return pl.pallas_call(
        outer,
        in_specs=[pl.BlockSpec(memory_space=pl.ANY)] + [pl.BlockSpec(memory_space=pltpu.VMEM) for _ in consts],
        out_specs=pl.BlockSpec(memory_space=pl.ANY),
        out_shape=jax.ShapeDtypeStruct(h.shape, F32),
        compiler_params=pltpu.CompilerParams(vmem_limit_bytes=VMEM_LIMIT_BYTES),
        name="ffn",
    )(h, *consts)


def kernel(x, positions, mix_pre_norm, w_in, ssm_a_re, ssm_a_im, ssm_log_dt, ssm_b_re, ssm_b_im, ssm_c_re, ssm_c_im, ssm_d, w_glu_val, w_glu_gate, w_ret_up, w_out, mix_post_norm, ffn_pre_norm, w_ffn_gate, w_ffn_up, w_ffn_down, ffn_post_norm):
    bsz, seq, d_model = x.shape
    depth = w_in.shape[0]
    pos_f = positions.astype(F32)
    inv_freq = ROPE_BASE ** (-jnp.arange(0, RET_QK_DIM, 2, dtype=F32) / RET_QK_DIM)
    invf = jnp.concatenate([inv_freq, inv_freq]).reshape(RET_QK_DIM, 1)
    ret_consts = _retention_constants()
    h = x
    for layer in range(depth):
        s5_ops = _s5_prep_call(ssm_a_re[layer], ssm_a_im[layer], ssm_log_dt[layer], ssm_b_re[layer],
                               ssm_b_im[layer], ssm_c_re[layer], ssm_c_im[layer], ssm_d[layer])
        wglu = jnp.concatenate([w_glu_val[layer], w_glu_gate[layer]], axis=1).astype(BF16)
        h = _mixer_call(h, pos_f, invf, mix_pre_norm[layer].reshape(1, d_model), w_in[layer].astype(BF16),
                        ret_consts, w_ret_up[layer].astype(BF16), s5_ops, wglu, w_out[layer].astype(BF16),
                        mix_post_norm[layer].reshape(1, d_model))
        h = _ffn_call(h.reshape(bsz * seq, d_model), ffn_pre_norm[layer].reshape(1, d_model),
                      w_ffn_gate[layer].astype(BF16), w_ffn_up[layer].astype(BF16),
                      w_ffn_down[layer].astype(BF16), ffn_post_norm[layer].reshape(1, d_model)
                      ).reshape(bsz, seq, d_model)
    return h
```
